```python
import jax, jax.numpy as jnp
from jax import lax
import numpy as np

D_MODEL = 2048
BATCH = 16
SEQ = 2048
DEPTH = 4

GRID_W = 64
EPS = 1e-6
D_FF = 5632

D_ATTN = D_MODEL // 2
D_SSD = D_MODEL - D_ATTN
D_MIX = D_ATTN + D_SSD

HEAD_DIM = 128
N_Q_HEADS = D_ATTN // HEAD_DIM
N_KV_HEADS = 2
Q_PER_KV = N_Q_HEADS // N_KV_HEADS
D_KV = N_KV_HEADS * HEAD_DIM
ROPE_AXIS_DIM = HEAD_DIM // 2
ROPE_THETA = 10000.0
Q_BLOCK = 128

SSD_HEAD_DIM = 64
N_SSD_HEADS = D_SSD // SSD_HEAD_DIM
N_GROUPS = 2
HEADS_PER_GROUP = N_SSD_HEADS // N_GROUPS
D_STATE = 128
D_CONV = 5
CHUNK = 128
CONV_DIM = D_SSD + 2 * N_GROUPS * D_STATE

D_IN_PROJ = D_ATTN + 2 * D_KV + D_SSD + CONV_DIM + 2 * N_SSD_HEADS
SPLIT_POINTS = (D_ATTN, D_ATTN + D_KV, D_ATTN + 2 * D_KV, D_ATTN + 2 * D_KV + D_SSD,
                D_ATTN + 2 * D_KV + D_SSD + CONV_DIM)

kernel_name = "hymba_macaron_ssd_axial_gqa_encoder"


def rmsnorm(x, g):
    xf = x.astype(jnp.float32)
    r = lax.rsqrt(jnp.mean(xf * xf, axis=-1, keepdims=True) + EPS)
    return (xf * r * g.astype(jnp.float32)).astype(x.dtype)


def swiglu(x, w_gu, w_down):
    g, u = jnp.split(x @ w_gu, 2, axis=-1)
    return (jax.nn.silu(g) * u) @ w_down


def axial_rope_tables(L):
    rows = L // GRID_W
    row_pos = jnp.repeat(jnp.arange(rows), GRID_W).astype(jnp.float32)
    col_pos = jnp.tile(jnp.arange(GRID_W), rows).astype(jnp.float32)
    inv_freq = ROPE_THETA ** (-jnp.arange(0, ROPE_AXIS_DIM, 2, dtype=jnp.float32) / ROPE_AXIS_DIM)
    ang_r = row_pos[:, None] * inv_freq
    ang_c = col_pos[:, None] * inv_freq
    return (jnp.cos(ang_r), jnp.sin(ang_r), jnp.cos(ang_c), jnp.sin(ang_c))


def _rotate(x, cos, sin):
    shape = (cos.shape[0],) + (1,) * (x.ndim - 3) + (cos.shape[-1],)
    cos, sin = cos.reshape(shape), sin.reshape(shape)
    x1, x2 = jnp.split(x, 2, axis=-1)
    return jnp.concatenate([x1 * cos - x2 * sin, x2 * cos + x1 * sin], axis=-1)


def apply_axial_rope(x, rope):
    cos_r, sin_r, cos_c, sin_c = rope
    xf = x.astype(jnp.float32)
    x_row, x_col = jnp.split(xf, 2, axis=-1)
    out = jnp.concatenate([_rotate(x_row, cos_r, sin_r), _rotate(x_col, cos_c, sin_c)], axis=-1)
    return out.astype(x.dtype)


def attention_group(q, k, v, q_gain, k_gain, out_gain, rope):
    b, L, _ = q.shape
    q = q.reshape(b, L, N_KV_HEADS, Q_PER_KV, HEAD_DIM)
    k = k.reshape(b, L, N_KV_HEADS, HEAD_DIM)
    v = v.reshape(b, L, N_KV_HEADS, HEAD_DIM)
    q = apply_axial_rope(rmsnorm(q, q_gain), rope)
    k = apply_axial_rope(rmsnorm(k, k_gain), rope)
    scale = HEAD_DIM ** -0.5
    nb = L // Q_BLOCK
    qb = q.reshape(b, nb, Q_BLOCK, N_KV_HEADS, Q_PER_KV, HEAD_DIM).transpose(1, 0, 2, 3, 4, 5)

    def one_block(q_blk):
        s = jnp.einsum("bqkrd,bskd->bkrqs", q_blk, k).astype(jnp.float32) * scale
        p = jax.nn.softmax(s, axis=-1)
        return jnp.einsum("bkrqs,bskd->bqkrd", p.astype(v.dtype), v)

    o = lax.map(one_block, qb)
    o = o.transpose(1, 0, 2, 3, 4, 5).reshape(b, L, D_ATTN)
    return rmsnorm(o, out_gain)


def segsum(a):
    T = a.shape[-1]
    cs = jnp.cumsum(a, axis=-1)
    diff = cs[..., :, None] - cs[..., None, :]
    mask = jnp.tril(jnp.ones((T, T), dtype=bool))
    return jnp.where(mask, diff, -jnp.inf)


def ssd_chunked(X, dtA, B, C):
    b, L, G, R, P = X.shape
    nc = L // CHUNK
    X = X.reshape(b, nc, CHUNK, G, R, P)
    B = B.reshape(b, nc, CHUNK, G, D_STATE)
    C = C.reshape(b, nc, CHUNK, G, D_STATE)
    A = dtA.reshape(b, nc, CHUNK, G, R).transpose(0, 3, 4, 1, 2)
    A_cs = jnp.cumsum(A, axis=-1)
    Lmat = jnp.exp(segsum(A))
    CB = jnp.einsum("bclgn,bcsgn->bgcls", C, B)
    y_diag = jnp.einsum("bgcls,bgrcls,bcsgrp->bclgrp", CB, Lmat, X)
    decay_states = jnp.exp(A_cs[..., -1:] - A_cs)
    states = jnp.einsum("bclgn,bgrcl,bclgrp->bcgrpn", B, decay_states, X)
    states = jnp.concatenate([jnp.zeros_like(states[:, :1]), states], axis=1)
    chunk_decay = jnp.exp(segsum(jnp.pad(A_cs[..., -1], ((0, 0),) * 3 + ((1, 0),))))
    states = jnp.einsum("bgrzc,bcgrpn->bzgrpn", chunk_decay, states)[:, :-1]
    y_off = jnp.einsum("bclgn,bcgrpn,bgrcl->bclgrp", C, states, jnp.exp(A_cs))
    return (y_diag + y_off).reshape(b, L, G, R, P)


def centred_depthwise_conv(u, w, bias):
    pad = (D_CONV - 1) // 2
    out = lax.conv_general_dilated(u, w[:, None, :].astype(u.dtype), window_strides=(1,),
                                   padding=[(pad, pad)], dimension_numbers=("NWC", "WIO", "NWC"),
                                   feature_group_count=u.shape[-1])
    return out + bias


def ssd_group(z, xBC, dt_raw, conv_w, conv_b, dt_bias, a_log, d_skip, norm_gain):
    b, L, _ = z.shape
    xBC = jax.nn.silu(centred_depthwise_conv(xBC, conv_w, conv_b)).astype(jnp.float32)
    xs, Bm, Cm = jnp.split(xBC, [D_SSD, D_SSD + N_GROUPS * D_STATE], axis=-1)
    xs = xs.reshape(b, L, N_GROUPS, HEADS_PER_GROUP, SSD_HEAD_DIM)
    Bm = Bm.reshape(b, L, N_GROUPS, D_STATE)
    Cm = Cm.reshape(b, L, N_GROUPS, D_STATE)
    dt = jax.nn.softplus(dt_raw.astype(jnp.float32).reshape(b, L, 2, N_SSD_HEADS)
                         + dt_bias.astype(jnp.float32))
    A = -jnp.exp(a_log.astype(jnp.float32))
    dtA = (dt * A).reshape(b, L, 2, N_GROUPS, HEADS_PER_GROUP)
    dt = dt.reshape(b, L, 2, N_GROUPS, HEADS_PER_GROUP)
    y_fwd = ssd_chunked(xs * dt[:, :, 0, ..., None], dtA[:, :, 0], Bm, Cm)
    flip = lambda t: jnp.flip(t, axis=1)
    y_bwd = flip(ssd_chunked(flip(xs * dt[:, :, 1, ..., None]), flip(dtA[:, :, 1]), flip(Bm), flip(Cm)))
    d = d_skip.astype(jnp.float32).reshape(N_GROUPS, HEADS_PER_GROUP)[..., None]
    y = (y_fwd + y_bwd + d * xs).reshape(b, L, N_GROUPS, D_SSD // N_GROUPS)
    y = y * jax.nn.silu(z.astype(jnp.float32)).reshape(b, L, N_GROUPS, D_SSD // N_GROUPS)
    y = y * lax.rsqrt(jnp.mean(y * y, axis=-1, keepdims=True) + EPS)
    return (y.reshape(b, L, D_SSD) * norm_gain.astype(jnp.float32)).astype(z.dtype)


def setup_inputs(seed: int = 0) -> dict:
    key = jax.random.key(seed)
    ks = jax.random.split(key, 24)
    f32 = jnp.float32
    nrm = lambda k, shape, s: jax.random.normal(k, shape, f32) * s
    gain = lambda k, shape: 1.0 + 0.02 * jax.random.normal(k, shape, f32)
    dt0 = jnp.exp(jax.random.uniform(ks[9], (DEPTH, 2, N_SSD_HEADS), f32,
                                     minval=math_log(1e-3), maxval=math_log(1e-1)))
    return {
        "x": jax.random.normal(ks[0], (BATCH, SEQ, D_MODEL), f32),
        "ffn1_norm": gain(ks[1], (DEPTH, D_MODEL)),
        "ffn1_w_gu": nrm(ks[2], (DEPTH, D_MODEL, 2 * D_FF), D_MODEL ** -0.5),
        "ffn1_w_down": nrm(ks[3], (DEPTH, D_FF, D_MODEL), D_FF ** -0.5),
        "mix_norm": gain(ks[4], (DEPTH, D_MODEL)),
        "w_in": nrm(ks[5], (DEPTH, D_MODEL, D_IN_PROJ), D_MODEL ** -0.5),
        "conv_w": nrm(ks[6], (DEPTH, D_CONV, CONV_DIM), D_CONV ** -0.5),
        "conv_b": nrm(ks[7], (DEPTH, CONV_DIM), 0.02),
        "dt_bias": dt0 + jnp.log(-jnp.expm1(-dt0)),
        "a_log": jnp.log(jax.random.uniform(ks[8], (DEPTH, 2, N_SSD_HEADS), f32, minval=1.0, maxval=16.0)),
        "d_skip": 1.0 + 0.1 * jax.random.normal(ks[10], (DEPTH, N_SSD_HEADS), f32),
        "q_norm": gain(ks[11], (DEPTH, HEAD_DIM)),
        "k_norm": gain(ks[12], (DEPTH, HEAD_DIM)),
        "attn_out_norm": gain(ks[13], (DEPTH, D_ATTN)),
        "ssd_out_norm": gain(ks[14], (DEPTH, D_SSD)),
        "w_out": nrm(ks[15], (DEPTH, D_MIX, D_MODEL), D_MIX ** -0.5),
        "ffn2_norm": gain(ks[16], (DEPTH, D_MODEL)),
        "ffn2_w_gu": nrm(ks[17], (DEPTH, D_MODEL, 2 * D_FF), D_MODEL ** -0.5),
        "ffn2_w_down": nrm(ks[18], (DEPTH, D_FF, D_MODEL), D_FF ** -0.5),
        "final_norm": gain(ks[19], (D_MODEL,)),
    }


def math_log(v):
    return float(np.log(v))


def reference(x, ffn1_norm, ffn1_w_gu, ffn1_w_down, mix_norm, w_in, conv_w, conv_b, dt_bias, a_log,
              d_skip, q_norm, k_norm, attn_out_norm, ssd_out_norm, w_out, ffn2_norm, ffn2_w_gu,
              ffn2_w_down, final_norm):
    L = x.shape[1]
    rope = axial_rope_tables(L)
    h = x
    for i in range(DEPTH):
        h = h + 0.5 * swiglu(rmsnorm(h, ffn1_norm[i]), ffn1_w_gu[i], ffn1_w_down[i])
        u = rmsnorm(h, mix_norm[i])
        q, k, v, z, xBC, dt_raw = jnp.split(u @ w_in[i], SPLIT_POINTS, axis=-1)
        a_out = attention_group(q, k, v, q_norm[i], k_norm[i], attn_out_norm[i], rope)
        s_out = ssd_group(z, xBC, dt_raw, conv_w[i], conv_b[i], dt_bias[i], a_log[i], d_skip[i],
                          ssd_out_norm[i])
        mixed = jnp.concatenate([a_out.astype(h.dtype), s_out.astype(h.dtype)], axis=-1)
        h = h + mixed @ w_out[i]
        h = h + 0.5 * swiglu(rmsnorm(h, ffn2_norm[i]), ffn2_w_gu[i], ffn2_w_down[i])
    return rmsnorm(h, final_norm).astype(x.dtype)
```

```python
import functools

import jax
import jax.numpy as jnp
from jax import lax
from jax.experimental import pallas as pl
from jax.experimental.pallas import tpu as pltpu

F32 = jnp.float32
BF16 = jnp.bfloat16

EPS = 1e-6
GRID_W = 64
ROPE_THETA = 10000.0
HEAD_DIM = 128
N_KV_HEADS = 2
SSD_HEAD_DIM = 64
N_GROUPS = 2
D_STATE = 128
D_CONV = 5
CHUNK = 128
LANES = 128
CONV_HALO = 8
VMEM_LIMIT = 56 * 1024 * 1024


def _cparams(*sem):
    return pltpu.CompilerParams(dimension_semantics=sem, vmem_limit_bytes=VMEM_LIMIT)


def _token_tile(seq_len):
    return min(512, seq_len)


def _rms(x, gain):
    r = lax.rsqrt(jnp.mean(x * x, axis=-1, keepdims=True) + EPS)
    return x * r * gain


def _ffn_kernel(x_ref, gain_ref, wg_ref, wu_ref, wd_ref, fgain_ref, o_ref, xn_ref, *, final_norm):
    j = pl.program_id(1)

    @pl.when(j == 0)
    def _():
        x = x_ref[...]
        xn_ref[...] = _rms(x, gain_ref[...]).astype(BF16)
        o_ref[...] = x

    xn = xn_ref[...]
    g = jnp.dot(xn, wg_ref[...], preferred_element_type=F32)
    u = jnp.dot(xn, wu_ref[...], preferred_element_type=F32)
    h = (g * jax.nn.sigmoid(g) * (0.5 * u)).astype(BF16)
    o_ref[...] += jnp.dot(h, wd_ref[...], preferred_element_type=F32)

    if final_norm:
        @pl.when(j == pl.num_programs(1) - 1)
        def _():
            o_ref[...] = _rms(o_ref[...], fgain_ref[...])


def _ffn(h, gain, w_gu, w_down, final_gain, *, tm, tf, final_norm):
    T, D = h.shape
    FF = w_down.shape[0]
    nj = FF // tf
    return pl.pallas_call(
        functools.partial(_ffn_kernel, final_norm=final_norm),
        out_shape=jax.ShapeDtypeStruct((T, D), F32),
        grid=(T // tm, nj),
        in_specs=[
            pl.BlockSpec((tm, D), lambda i, j: (i, 0)),
            pl.BlockSpec((1, D), lambda i, j: (0, 0)),
            pl.BlockSpec((D, tf), lambda i, j: (0, j)),
            pl.BlockSpec((D, tf), lambda i, j: (0, j + nj)),
            pl.BlockSpec((tf, D), lambda i, j: (j, 0)),
            pl.BlockSpec((1, D), lambda i, j: (0, 0)),
        ],
        out_specs=pl.BlockSpec((tm, D), lambda i, j: (i, 0)),
        scratch_shapes=[pltpu.VMEM((tm, D), BF16)],
        compiler_params=_cparams("parallel", "arbitrary"),
        name="ffn_final" if final_norm else "ffn",
    )(h, gain, w_gu, w_gu, w_down, final_gain)


def _rope(x, cos, sin_signed):
    lane = lax.broadcasted_iota(jnp.int32, x.shape, 1)
    first_half = (lane % 64) < 32
    partner = jnp.where(first_half, pltpu.roll(x, LANES - 32, 1), pltpu.roll(x, 32, 1))
    return x * cos + partner * sin_signed


def _qkv_kernel(h_ref, gain_ref, w_ref, cos_ref, sin_ref, qg_ref, kg_ref,
                q_ref, k_ref, v_ref, *, n_q, n_kv, scale):
    xn = _rms(h_ref[...], gain_ref[...]).astype(BF16)
    p = jnp.dot(xn, w_ref[...], preferred_element_type=F32)
    cos = cos_ref[...]
    sin = sin_ref[...]
    for hd in range(n_q):
        qh = p[:, hd * HEAD_DIM:(hd + 1) * HEAD_DIM]
        qh = _rope(_rms(qh, qg_ref[...]), cos, sin) * scale
        q_ref[:, hd * HEAD_DIM:(hd + 1) * HEAD_DIM] = qh.astype(BF16)
    off = n_q * HEAD_DIM
    for hd in range(n_kv):
        kh = p[:, off + hd * HEAD_DIM:off + (hd + 1) * HEAD_DIM]
        kh = _rope(_rms(kh, kg_ref[...]), cos, sin)
        k_ref[:, hd * HEAD_DIM:(hd + 1) * HEAD_DIM] = kh.astype(BF16)
    off += n_kv * HEAD_DIM
    v_ref[...] = p[:, off:off + n_kv * HEAD_DIM].astype(BF16)


def _qkv_proj(h, gain, w_qkv, cos, sin, q_gain, k_gain, *, tm, seq_len, n_q, n_kv):
    T, D = h.shape
    tiles_per_seq = seq_len // tm
    dq, dkv = n_q * HEAD_DIM, n_kv * HEAD_DIM
    row = lambda i: (i, 0)
    const = lambda i: (0, 0)
    pos = lambda i: (i % tiles_per_seq, 0)
    return pl.pallas_call(
        functools.partial(_qkv_kernel, n_q=n_q, n_kv=n_kv, scale=HEAD_DIM ** -0.5),
        out_shape=(jax.ShapeDtypeStruct((T, dq), BF16),
                   jax.ShapeDtypeStruct((T, dkv), BF16),
                   jax.ShapeDtypeStruct((T, dkv), BF16)),
        grid=(T // tm,),
        in_specs=[
            pl.BlockSpec((tm, D), row),
            pl.BlockSpec((1, D), const),
            pl.BlockSpec((D, dq + 2 * dkv), const),
            pl.BlockSpec((tm, HEAD_DIM), pos),
            pl.BlockSpec((tm, HEAD_DIM), pos),
            pl.BlockSpec((1, HEAD_DIM), const),
            pl.BlockSpec((1, HEAD_DIM), const),
        ],
        out_specs=(pl.BlockSpec((tm, dq), row),
                   pl.BlockSpec((tm, dkv), row),
                   pl.BlockSpec((tm, dkv), row)),
        compiler_params=_cparams("parallel"),
        name="qkv_proj",
    )(h, gain, w_qkv, cos, sin, q_gain, k_gain)


def _ssd_proj_kernel(h_ref, gain_ref, w_ref, z_ref, xbc_ref, dt_ref, *, d_z, d_xbc):
    xn = _rms(h_ref[...], gain_ref[...]).astype(BF16)
    p = jnp.dot(xn, w_ref[...], preferred_element_type=F32)
    z_ref[...] = p[:, :d_z]
    xbc_ref[...] = p[:, d_z:d_z + d_xbc]
    dt_ref[...] = p[:, d_z + d_xbc:]


def _ssd_proj(h, gain, w_ssd, *, tm, d_z, d_xbc, d_dt):
    T, D = h.shape
    row = lambda i: (i, 0)
    const = lambda i: (0, 0)
    return pl.pallas_call(
        functools.partial(_ssd_proj_kernel, d_z=d_z, d_xbc=d_xbc),
        out_shape=(jax.ShapeDtypeStruct((T, d_z), F32),
                   jax.ShapeDtypeStruct((T, d_xbc), F32),
                   jax.ShapeDtypeStruct((T, d_dt), F32)),
        grid=(T // tm,),
        in_specs=[
            pl.BlockSpec((tm, D), row),
            pl.BlockSpec((1, D), const),
            pl.BlockSpec((D, d_z + d_xbc + d_dt), const),
        ],
        out_specs=(pl.BlockSpec((tm, d_z), row),
                   pl.BlockSpec((tm, d_xbc), row),
                   pl.BlockSpec((tm, d_dt), row)),
        compiler_params=_cparams("parallel"),
        name="ssd_proj",
    )(h, gain, w_ssd)


def _attn_kernel(q_ref, k_ref, v_ref, gain_ref, o_ref, acc_ref, *, n_q, n_kv):
    q_per_kv = n_q // n_kv
    for hd in range(n_q):
        kv = hd // q_per_kv
        qh = q_ref[:, hd * HEAD_DIM:(hd + 1) * HEAD_DIM]
        kh = k_ref[:, kv * HEAD_DIM:(kv + 1) * HEAD_DIM]
        vh = v_ref[:, kv * HEAD_DIM:(kv + 1) * HEAD_DIM]
        s = lax.dot_general(qh, kh, (((1,), (1,)), ((), ())), preferred_element_type=F32)
        m = jnp.max(s, axis=-1, keepdims=True)
        p = jnp.exp(s - m)
        l = jnp.sum(p, axis=-1, keepdims=True)
        o = jnp.dot(p.astype(BF16), vh, preferred_element_type=F32)
        acc_ref[:, hd * HEAD_DIM:(hd + 1) * HEAD_DIM] = o / l
    o_ref[...] = _rms(acc_ref[...], gain_ref[...]).astype(BF16)


def _attention(q, k, v, gain, *, batch, seq_len, tq, n_q, n_kv):
    T, dq = q.shape
    dkv = k.shape[1]
    nq_tiles = seq_len // tq
    return pl.pallas_call(
        functools.partial(_attn_kernel, n_q=n_q, n_kv=n_kv),
        out_shape=jax.ShapeDtypeStruct((T, dq), BF16),
        grid=(batch, nq_tiles),
        in_specs=[
            pl.BlockSpec((tq, dq), lambda b, i: (b * nq_tiles + i, 0)),
            pl.BlockSpec((seq_len, dkv), lambda b, i: (b, 0)),
            pl.BlockSpec((seq_len, dkv), lambda b, i: (b, 0)),
            pl.BlockSpec((1, dq), lambda b, i: (0, 0)),
        ],
        out_specs=pl.BlockSpec((tq, dq), lambda b, i: (b * nq_tiles + i, 0)),
        scratch_shapes=[pltpu.VMEM((tq, dq), F32)],
        compiler_params=_cparams("parallel", "arbitrary"),
        name="attention",
    )(q, k, v, gain)


def _split3(a):
    hi = a.astype(BF16)
    r1 = a - hi.astype(F32)
    mid = r1.astype(BF16)
    lo = (r1 - mid.astype(F32)).astype(BF16)
    return hi, mid, lo


def _dot_sel_rhs(a, sel):
    return sum(jnp.dot(part, sel, preferred_element_type=F32) for part in _split3(a))


def _dot_sel_lhs(sel, a):
    return sum(jnp.dot(sel, part, preferred_element_type=F32) for part in _split3(a))


def _softplus(x):
    return jnp.maximum(x, 0.0) + jnp.log1p(jnp.exp(-jnp.abs(x)))


def _silu(x):
    return x * jax.nn.sigmoid(x)


def _conv_chunk(u_ref, w_ref, b_ref, c, n_chunks):
    seq_len = n_chunks * CHUNK
    r0 = pl.multiple_of(c * CHUNK, CHUNK)
    prev0 = pl.multiple_of(jnp.maximum(r0 - CONV_HALO, 0), CONV_HALO)
    next0 = pl.multiple_of(jnp.minimum(r0 + CHUNK, seq_len - CONV_HALO), CONV_HALO)
    prev = u_ref[pl.ds(prev0, CONV_HALO), :] * (c > 0).astype(F32)
    nxt = u_ref[pl.ds(next0, CONV_HALO), :] * (c < n_chunks - 1).astype(F32)
    ext = jnp.concatenate([prev, u_ref[pl.ds(r0, CHUNK), :], nxt], axis=0)
    rows = CHUNK + 2 * CONV_HALO
    pad = (D_CONV - 1) // 2
    acc = None
    for tap in range(D_CONV):
        shift = pad - tap
        shifted = ext if shift == 0 else pltpu.roll(ext, shift % rows, 0)
        term = shifted[CONV_HALO:CONV_HALO + CHUNK] * w_ref[tap:tap + 1, :]
        acc = term if acc is None else acc + term
    return _silu(acc + b_ref[...])


def _ssd_kernel(x_ref, b_ref, c_ref, z_ref, dt_ref,
                wx_ref, wb_ref, wc_ref, bx_ref, bb_ref, bc_ref,
                dtb_ref, alog_ref, dskip_ref, gain_ref,
                o_ref,
                xs_s, bm_s, cm_s, yf_s, st_s, *, n_chunks, heads):
    width = heads * SSD_HEAD_DIM
    pairs = width // LANES

    def conv_body(c, carry):
        r0 = pl.multiple_of(c * CHUNK, CHUNK)
        xs_s[pl.ds(r0, CHUNK), :] = _conv_chunk(x_ref, wx_ref, bx_ref, c, n_chunks)
        bm_s[pl.ds(r0, CHUNK), :] = _conv_chunk(b_ref, wb_ref, bb_ref, c, n_chunks).astype(BF16)
        cm_s[pl.ds(r0, CHUNK), :] = _conv_chunk(c_ref, wc_ref, bc_ref, c, n_chunks).astype(BF16)
        return carry

    lax.fori_loop(0, n_chunks, conv_body, 0)

    row = lax.broadcasted_iota(jnp.int32, (CHUNK, CHUNK), 0)
    col = lax.broadcasted_iota(jnp.int32, (CHUNK, CHUNK), 1)
    lane_w = lax.broadcasted_iota(jnp.int32, (LANES, width), 1) // SSD_HEAD_DIM
    row_w = lax.broadcasted_iota(jnp.int32, (LANES, width), 0)
    lane_x = lax.broadcasted_iota(jnp.int32, (CHUNK, LANES), 1)
    neg_a = -jnp.exp(alog_ref[...])

    def scan_chunk(c, direction):
        fwd = direction == 0
        lane0 = 0 if fwd else heads
        r0 = pl.multiple_of(c * CHUNK, CHUNK)
        dt = _softplus(dt_ref[pl.ds(r0, CHUNK), :] + dtb_ref[...])
        a = dt * neg_a
        keep = (col <= row) if fwd else (col >= row)
        tri = keep.astype(BF16)
        cs = _dot_sel_lhs(tri, a)
        cs_t = _dot_sel_rhs(a.T, tri.T)
        expand = (row_w == lane_w + lane0).astype(BF16)
        cs_e = _dot_sel_rhs(cs, expand)
        dt_e = _dot_sel_rhs(dt, expand)
        last = CHUNK - 1 if fwd else 0
        total_e = cs_e[last:last + 1, :]

        xs = xs_s[pl.ds(r0, CHUNK), :]
        bm = bm_s[pl.ds(r0, CHUNK), :]
        cm = cm_s[pl.ds(r0, CHUNK), :]
        x_dt = xs * dt_e
        x_bf = x_dt.astype(BF16)
        cb = lax.dot_general(cm, bm, (((1,), (1,)), ((), ())), preferred_element_type=F32)

        state = st_s[...]
        y_off = jnp.dot(cm, state.astype(BF16), preferred_element_type=F32) * jnp.exp(cs_e)

        pieces = []
        for p in range(pairs):
            xp = x_bf[:, p * LANES:(p + 1) * LANES]
            zero = jnp.zeros_like(xp)
            x_bd = jnp.concatenate([jnp.where(lane_x < SSD_HEAD_DIM, xp, zero),
                                    jnp.where(lane_x >= SSD_HEAD_DIM, xp, zero)], axis=0)
            ms = []
            for hh in (2 * p, 2 * p + 1):
                j = lane0 + hh
                diff = cs[:, j:j + 1] - cs_t[j:j + 1, :]
                lm = jnp.exp(jnp.where(keep, diff, -jnp.inf))
                ms.append((cb * lm).astype(BF16))
            pieces.append(jnp.dot(jnp.concatenate(ms, axis=1), x_bd, preferred_element_type=F32))
        y = jnp.concatenate(pieces, axis=1) + y_off

        x_dec = (x_dt * jnp.exp(total_e - cs_e)).astype(BF16)
        new = lax.dot_general(bm, x_dec, (((0,), (0,)), ((), ())), preferred_element_type=F32)
        st_s[...] = state * jnp.exp(total_e) + new
        return y

    st_s[...] = jnp.zeros_like(st_s)

    def fwd_body(c, carry):
        r0 = pl.multiple_of(c * CHUNK, CHUNK)
        yf_s[pl.ds(r0, CHUNK), :] = scan_chunk(c, 0)
        return carry

    lax.fori_loop(0, n_chunks, fwd_body, 0)

    st_s[...] = jnp.zeros_like(st_s)

    def bwd_body(i, carry):
        c = n_chunks - 1 - i
        r0 = pl.multiple_of(c * CHUNK, CHUNK)
        y = yf_s[pl.ds(r0, CHUNK), :] + scan_chunk(c, 1)
        y = y + dskip_ref[...] * xs_s[pl.ds(r0, CHUNK), :]
        y = y * _silu(z_ref[pl.ds(r0, CHUNK), :])
        o_ref[pl.ds(r0, CHUNK), :] = _rms(y, gain_ref[...]).astype(BF16)
        return carry

    lax.fori_loop(0, n_chunks, bwd_body, 0)


def _ssd(z, xbc, dt, conv_w, conv_b, dt_bias, a_log, d_skip, gain, *, batch, seq_len):
    T, d_ssd = z.shape
    width = d_ssd // N_GROUPS
    heads = width // SSD_HEAD_DIM
    n_chunks = seq_len // CHUNK
    xblk = d_ssd // width
    b0 = d_ssd // D_STATE
    c0 = b0 + N_GROUPS

    def seq(cols, first):
        return pl.BlockSpec((seq_len, cols), lambda b, g: (b, first + g))

    def par(rows, cols, first):
        return pl.BlockSpec((rows, cols), lambda b, g: (0, first + g))

    return pl.pallas_call(
        functools.partial(_ssd_kernel, n_chunks=n_chunks, heads=heads),
        out_shape=jax.ShapeDtypeStruct((T, d_ssd), BF16),
        grid=(batch, N_GROUPS),
        in_specs=[
            seq(width, 0), seq(D_STATE, b0), seq(D_STATE, c0), seq(width, 0), seq(LANES, 0),
            par(D_CONV, width, 0), par(D_CONV, D_STATE, b0), par(D_CONV, D_STATE, c0),
            par(1, width, 0), par(1, D_STATE, b0), par(1, D_STATE, c0),
            par(1, LANES, 0), par(1, LANES, 0), par(1, width, 0), par(1, width, 0),
        ],
        out_specs=seq(width, 0),
        scratch_shapes=[
            pltpu.VMEM((seq_len, width), F32),
            pltpu.VMEM((seq_len, D_STATE), BF16),
            pltpu.VMEM((seq_len, D_STATE), BF16),
            pltpu.VMEM((seq_len, width), F32),
            pltpu.VMEM((D_STATE, width), F32),
        ],
        compiler_params=_cparams("parallel", "arbitrary"),
        name="ssd",
    )(xbc, xbc, xbc, z, dt, conv_w, conv_w, conv_w, conv_b, conv_b, conv_b,
      dt_bias, a_log, d_skip, gain)


def _out_kernel(a_ref, s_ref, h_ref, wa_ref, ws_ref, o_ref):
    o_ref[...] = (h_ref[...]
                  + jnp.dot(a_ref[...], wa_ref[...], preferred_element_type=F32)
                  + jnp.dot(s_ref[...], ws_ref[...], preferred_element_type=F32))


def _out_proj(a, s, h, w_out, *, tm):
    T, D = h.shape
    da, ds = a.shape[1], s.shape[1]
    row = lambda i: (i, 0)
    return pl.pallas_call(
        _out_kernel,
        out_shape=jax.ShapeDtypeStruct((T, D), F32),
        grid=(T // tm,),
        in_specs=[
            pl.BlockSpec((tm, da), row),
            pl.BlockSpec((tm, ds), row),
            pl.BlockSpec((tm, D), row),
            pl.BlockSpec((da, D), lambda i: (0, 0)),
            pl.BlockSpec((ds, D), lambda i: (da // ds, 0)),
        ],
        out_specs=pl.BlockSpec((tm, D), row),
        compiler_params=_cparams("parallel"),
        name="out_proj",
    )(a, s, h, w_out, w_out)


def _rope_tables(seq_len):
    axis_dim = HEAD_DIM // 2
    t = jnp.arange(seq_len)
    row_pos = (t // GRID_W).astype(F32)
    col_pos = (t % GRID_W).astype(F32)
    inv_freq = ROPE_THETA ** (-jnp.arange(0, axis_dim, 2, dtype=F32) / axis_dim)
    ang_r = row_pos[:, None] * inv_freq
    ang_c = col_pos[:, None] * inv_freq
    cos = jnp.concatenate([jnp.cos(ang_r), jnp.cos(ang_r), jnp.cos(ang_c), jnp.cos(ang_c)], axis=-1)
    sin = jnp.concatenate([-jnp.sin(ang_r), jnp.sin(ang_r), -jnp.sin(ang_c), jnp.sin(ang_c)], axis=-1)
    return cos, sin


def _group_dt_columns(t, n_heads):
    hpg = n_heads // N_GROUPS
    lead = t.shape[:-1]
    t = t.reshape(lead + (2, N_GROUPS, hpg))
    t = jnp.moveaxis(t, -3, -2).reshape(lead + (N_GROUPS, 2 * hpg))
    t = jnp.pad(t, [(0, 0)] * (len(lead) + 1) + [(0, LANES - 2 * hpg)])
    return t.reshape(lead + (N_GROUPS * LANES,))


def kernel(x, ffn1_norm, ffn1_w_gu, ffn1_w_down, mix_norm, w_in, conv_w, conv_b, dt_bias, a_log, d_skip, q_norm, k_norm, attn_out_norm, ssd_out_norm, w_out, ffn2_norm, ffn2_w_gu, ffn2_w_down, final_norm):
    batch, seq_len, d_model = x.shape
    depth = ffn1_norm.shape[0]
    d_attn = attn_out_norm.shape[1]
    d_ssd = ssd_out_norm.shape[1]
    n_ssd_heads = d_skip.shape[1]
    n_q = d_attn // HEAD_DIM
    d_kv = N_KV_HEADS * HEAD_DIM
    d_xbc = conv_w.shape[2]
    d_qkv = d_attn + 2 * d_kv
    d_dt = N_GROUPS * LANES

    tm = _token_tile(seq_len)
    tf = 512
    T = batch * seq_len
    h = x.reshape(T, d_model)
    cos, sin = _rope_tables(seq_len)
    row = lambda v: v.reshape(1, -1)
    final_gain = row(final_norm)

    for i in range(depth):
        last = i == depth - 1
        h = _ffn(h, row(ffn1_norm[i]), ffn1_w_gu[i].astype(BF16), ffn1_w_down[i].astype(BF16),
                 final_gain, tm=tm, tf=tf, final_norm=False)

        w_l = w_in[i]
        w_qkv = w_l[:, :d_qkv].astype(BF16)
        w_ssd = jnp.concatenate(
            [w_l[:, d_qkv:d_qkv + d_ssd + d_xbc],
             _group_dt_columns(w_l[:, d_qkv + d_ssd + d_xbc:], n_ssd_heads)], axis=1).astype(BF16)
        gain = row(mix_norm[i])
        q, k, v = _qkv_proj(h, gain, w_qkv, cos, sin, row(q_norm[i]), row(k_norm[i]),
                            tm=tm, seq_len=seq_len, n_q=n_q, n_kv=N_KV_HEADS)
        z, xbc, dt = _ssd_proj(h, gain, w_ssd, tm=tm, d_z=d_ssd, d_xbc=d_xbc, d_dt=d_dt)
        a_out = _attention(q, k, v, row(attn_out_norm[i]), batch=batch, seq_len=seq_len,
                           tq=tm, n_q=n_q, n_kv=N_KV_HEADS)
        s_out = _ssd(z, xbc, dt, conv_w[i], row(conv_b[i]),
                     row(_group_dt_columns(dt_bias[i].reshape(-1), n_ssd_heads)),
                     row(_group_dt_columns(a_log[i].reshape(-1), n_ssd_heads)),
                     row(jnp.repeat(d_skip[i], SSD_HEAD_DIM)), row(ssd_out_norm[i]),
                     batch=batch, seq_len=seq_len)
        h = _out_proj(a_out, s_out, h, w_out[i].astype(BF16), tm=tm)

        h = _ffn(h, row(ffn2_norm[i]), ffn2_w_gu[i].astype(BF16), ffn2_w_down[i].astype(BF16),
                 final_gain, tm=tm, tf=tf, final_norm=last)
    return h.reshape(batch, seq_len, d_model)
```

```python
import functools
import math

import jax
import jax.numpy as jnp
from jax import lax
from jax.experimental import pallas as pl
from jax.experimental.pallas import tpu as pltpu

F32 = jnp.float32
BF16 = jnp.bfloat16

EPS = 1e-6
GRID_W = 64
ROPE_THETA = 10000.0
HEAD_DIM = 128
N_KV_HEADS = 2
SSD_HEAD_DIM = 64
N_GROUPS = 2
D_STATE = 128
D_CONV = 5
CHUNK = 128
LANES = 128
SUBLANES = 8
MXU_WIDTH = 256
CONV_HALO = SUBLANES
LOG2E = math.log2(math.e)
VMEM_LIMIT = 56 * 1024 * 1024


def _cparams(*sem):
    return pltpu.CompilerParams(dimension_semantics=sem, vmem_limit_bytes=VMEM_LIMIT)


def _token_tile(seq_len):
    return min(512, seq_len)


def _ffn_tiles(n_tokens, d_ff):
    tm = 1024 if n_tokens % 1024 == 0 else 512
    tf = 512 if d_ff % 512 == 0 else LANES
    return tm, tf


def _rms(x, gain):
    r = lax.rsqrt(jnp.mean(x * x, axis=-1, keepdims=True) + EPS)
    return x * r * gain


def _ffn_kernel(x_hbm, gain_ref, wg_ref, wu_ref, wd_ref, fgain_ref, o_ref, x_s, xn_ref, sem,
                *, tm, final_norm):
    i = pl.program_id(0)
    j = pl.program_id(1)

    def x_copy(tile):
        return pltpu.make_async_copy(x_hbm.at[pl.ds(tile * tm, tm), :], x_s, sem)

    @pl.when(j == 0)
    def _():
        @pl.when(i == 0)
        def _():
            x_copy(0).start()

        x_copy(i).wait()
        x = x_s[...]
        xn_ref[...] = _rms(x, gain_ref[...]).astype(BF16)
        o_ref[...] = x

        @pl.when(i + 1 < pl.num_programs(0))
        def _():
            x_copy(i + 1).start()

    xn = xn_ref[...]
    g = jnp.dot(xn, wg_ref[...], preferred_element_type=F32)
    u = jnp.dot(xn, wu_ref[...], preferred_element_type=F32)
    h = (g * jax.nn.sigmoid(g) * (0.5 * u)).astype(BF16)
    o_ref[...] += jnp.dot(h, wd_ref[...], preferred_element_type=F32)

    if final_norm:
        @pl.when(j == pl.num_programs(1) - 1)
        def _():
            o_ref[...] = _rms(o_ref[...], fgain_ref[...])


def _ffn(h, gain, w_gu, w_down, final_gain, *, layer, tm, tf, final_norm):
    T, D = h.shape
    FF = w_down.shape[1]
    nj = FF // tf
    return pl.pallas_call(
        functools.partial(_ffn_kernel, tm=tm, final_norm=final_norm),
        out_shape=jax.ShapeDtypeStruct((T, D), F32),
        grid=(T // tm, nj),
        in_specs=[
            pl.BlockSpec(memory_space=pl.ANY),
            pl.BlockSpec((1, D), lambda i, j: (0, 0)),
            pl.BlockSpec((None, D, tf), lambda i, j: (layer, 0, j)),
            pl.BlockSpec((None, D, tf), lambda i, j: (layer, 0, j + nj)),
            pl.BlockSpec((None, tf, D), lambda i, j: (layer, j, 0)),
            pl.BlockSpec((1, D), lambda i, j: (0, 0)),
        ],
        out_specs=pl.BlockSpec((tm, D), lambda i, j: (i, 0)),
        scratch_shapes=[pltpu.VMEM((tm, D), F32), pltpu.VMEM((tm, D), BF16), pltpu.SemaphoreType.DMA(())],
        compiler_params=_cparams("arbitrary", "arbitrary"),
        name="ffn_final" if final_norm else "ffn",
    )(h, gain, w_gu, w_gu, w_down, final_gain)


def _rope(x, cos, sin_signed):
    return x * cos + pltpu.roll(x, HEAD_DIM // 2, 1) * sin_signed


def _qkv_kernel(h_ref, gain_ref, w_ref, cos_ref, sin_ref, qg_ref, kg_ref,
                q_ref, k_ref, v_ref, *, n_q, n_kv, q_scale):
    xn = _rms(h_ref[...], gain_ref[...]).astype(BF16)
    cos = cos_ref[...]
    sin = sin_ref[...]
    heads_per_dot = MXU_WIDTH // HEAD_DIM
    n_heads = n_q + 2 * n_kv
    for g in range(n_heads // heads_per_dot):
        c0 = g * MXU_WIDTH
        p = jnp.dot(xn, w_ref[:, c0:c0 + MXU_WIDTH], preferred_element_type=F32)
        for sub in range(heads_per_dot):
            hd = g * heads_per_dot + sub
            ph = p[:, sub * HEAD_DIM:(sub + 1) * HEAD_DIM]
            if hd < n_q:
                qh = _rope(_rms(ph, qg_ref[...]), cos, sin) * q_scale
                q_ref[:, hd * HEAD_DIM:(hd + 1) * HEAD_DIM] = qh.astype(BF16)
            elif hd < n_q + n_kv:
                kh = _rope(_rms(ph, kg_ref[...]), cos, sin)
                kk = hd - n_q
                k_ref[:, kk * HEAD_DIM:(kk + 1) * HEAD_DIM] = kh.astype(BF16)
            else:
                vv = hd - n_q - n_kv
                v_ref[:, vv * HEAD_DIM:(vv + 1) * HEAD_DIM] = ph.astype(BF16)


def _qkv_proj(h, gain, w_qkv, cos, sin, q_gain, k_gain, *, layer, tm, seq_len, n_q, n_kv):
    T, D = h.shape
    tiles_per_seq = seq_len // tm
    dq, dkv = n_q * HEAD_DIM, n_kv * HEAD_DIM
    row = lambda i: (i, 0)
    const = lambda i: (0, 0)
    pos = lambda i: (i % tiles_per_seq, 0)
    return pl.pallas_call(
        functools.partial(_qkv_kernel, n_q=n_q, n_kv=n_kv, q_scale=HEAD_DIM ** -0.5 * LOG2E),
        out_shape=(jax.ShapeDtypeStruct((T, dq), BF16),
                   jax.ShapeDtypeStruct((T, dkv), BF16),
                   jax.ShapeDtypeStruct((T, dkv), BF16)),
        grid=(T // tm,),
        in_specs=[
            pl.BlockSpec((tm, D), row),
            pl.BlockSpec((1, D), const),
            pl.BlockSpec((None, D, dq + 2 * dkv), lambda i: (layer, 0, 0)),
            pl.BlockSpec((tm, HEAD_DIM), pos),
            pl.BlockSpec((tm, HEAD_DIM), pos),
            pl.BlockSpec((1, HEAD_DIM), const),
            pl.BlockSpec((1, HEAD_DIM), const),
        ],
        out_specs=(pl.BlockSpec((tm, dq), row),
                   pl.BlockSpec((tm, dkv), row),
                   pl.BlockSpec((tm, dkv), row)),
        compiler_params=_cparams("parallel"),
        name="qkv_proj",
    )(h, gain, w_qkv, cos, sin, q_gain, k_gain)


def _ssd_proj_kernel(h_ref, gain_ref, w_ref, z_ref, xbc_ref, dt_ref, *, d_z, d_xbc):
    xn = _rms(h_ref[...], gain_ref[...]).astype(BF16)
    p = jnp.dot(xn, w_ref[...], preferred_element_type=F32)
    z_ref[...] = p[:, :d_z]
    xbc_ref[...] = p[:, d_z:d_z + d_xbc]
    dt_ref[...] = p[:, d_z + d_xbc:]


def _ssd_proj(h, gain, w_ssd, *, layer, tm, d_z, d_xbc, d_dt):
    T, D = h.shape
    row = lambda i: (i, 0)
    const = lambda i: (0, 0)
    return pl.pallas_call(
        functools.partial(_ssd_proj_kernel, d_z=d_z, d_xbc=d_xbc),
        out_shape=(jax.ShapeDtypeStruct((T, d_z), F32),
                   jax.ShapeDtypeStruct((T, d_xbc), F32),
                   jax.ShapeDtypeStruct((T, d_dt), F32)),
        grid=(T // tm,),
        in_specs=[
            pl.BlockSpec((tm, D), row),
            pl.BlockSpec((1, D), const),
            pl.BlockSpec((None, D, d_z + d_xbc + d_dt), lambda i: (layer, 0, 0)),
        ],
        out_specs=(pl.BlockSpec((tm, d_z), row),
                   pl.BlockSpec((tm, d_xbc), row),
                   pl.BlockSpec((tm, d_dt), row)),
        compiler_params=_cparams("parallel"),
        name="ssd_proj",
    )(h, gain, w_ssd)


def _attn_kernel(q_ref, k_ref, v_ref, gain_ref, o_ref, acc_ref, *, n_q, n_kv):
    q_per_kv = n_q // n_kv
    for hd in range(n_q):
        kv = hd // q_per_kv
        qh = q_ref[:, hd * HEAD_DIM:(hd + 1) * HEAD_DIM]
        kh = k_ref[:, kv * HEAD_DIM:(kv + 1) * HEAD_DIM]
        vh = v_ref[:, kv * HEAD_DIM:(kv + 1) * HEAD_DIM]
        s = lax.dot_general(qh, kh, (((1,), (1,)), ((), ())), preferred_element_type=F32)
        m = jnp.max(s, axis=-1, keepdims=True)
        p = jnp.exp2(s - m)
        l = jnp.sum(p, axis=-1, keepdims=True)
        o = jnp.dot(p.astype(BF16), vh, preferred_element_type=F32)
        acc_ref[:, hd * HEAD_DIM:(hd + 1) * HEAD_DIM] = o * (1.0 / l)
    o_ref[...] = _rms(acc_ref[...], gain_ref[...]).astype(BF16)


def _attention(q, k, v, gain, *, batch, seq_len, tq, n_q, n_kv):
    T, dq = q.shape
    dkv = k.shape[1]
    nq_tiles = seq_len // tq
    return pl.pallas_call(
        functools.partial(_attn_kernel, n_q=n_q, n_kv=n_kv),
        out_shape=jax.ShapeDtypeStruct((T, dq), BF16),
        grid=(batch, nq_tiles),
        in_specs=[
            pl.BlockSpec((tq, dq), lambda b, i: (b * nq_tiles + i, 0)),
            pl.BlockSpec((seq_len, dkv), lambda b, i: (b, 0)),
            pl.BlockSpec((seq_len, dkv), lambda b, i: (b, 0)),
            pl.BlockSpec((1, dq), lambda b, i: (0, 0)),
        ],
        out_specs=pl.BlockSpec((tq, dq), lambda b, i: (b * nq_tiles + i, 0)),
        scratch_shapes=[pltpu.VMEM((tq, dq), F32)],
        compiler_params=_cparams("parallel", "arbitrary"),
        name="attention",
    )(q, k, v, gain)


def _split3(a):
    hi = a.astype(BF16)
    r1 = a - hi.astype(F32)
    mid = r1.astype(BF16)
    lo = (r1 - mid.astype(F32)).astype(BF16)
    return hi, mid, lo


def _softplus(x):
    return jnp.maximum(x, 0.0) + jnp.log1p(jnp.exp(-jnp.abs(x)))


def _silu(x):
    return x * jax.nn.sigmoid(x)


def _conv_chunk(u_ref, w_ref, b_ref, c, n_chunks):
    seq_len = n_chunks * CHUNK
    r0 = pl.multiple_of(c * CHUNK, CHUNK)
    prev0 = pl.multiple_of(jnp.maximum(r0 - CONV_HALO, 0), CONV_HALO)
    next0 = pl.multiple_of(jnp.minimum(r0 + CHUNK, seq_len - CONV_HALO), CONV_HALO)
    prev = u_ref[pl.ds(prev0, CONV_HALO), :] * jnp.where(c > 0, 1.0, 0.0)
    nxt = u_ref[pl.ds(next0, CONV_HALO), :] * jnp.where(c < n_chunks - 1, 1.0, 0.0)
    ext = jnp.concatenate([prev, u_ref[pl.ds(r0, CHUNK), :], nxt], axis=0)
    rows = CHUNK + 2 * CONV_HALO
    pad = (D_CONV - 1) // 2
    acc = None
    for tap in range(D_CONV):
        shift = pad - tap
        shifted = ext if shift == 0 else pltpu.roll(ext, shift % rows, 0)
        term = shifted[CONV_HALO:CONV_HALO + CHUNK] * w_ref[tap:tap + 1, :]
        acc = term if acc is None else acc + term
    return _silu(acc + b_ref[...])


def _ssd_kernel(x_ref, b_ref, c_ref, z_ref, dt_ref,
                wx_ref, wb_ref, wc_ref, bx_ref, bb_ref, bc_ref,
                dtb_ref, alog_ref, dskip_ref, gain_ref,
                o_ref,
                cm_s, bmt_s, y_s, xd_s, ecs_s, dec_s, st_s, *, n_chunks, heads):
    width = heads * SSD_HEAD_DIM
    pairs = width // LANES
    dec_rows = 2 * SUBLANES

    row = lax.broadcasted_iota(jnp.int32, (CHUNK, CHUNK), 0)
    col = lax.broadcasted_iota(jnp.int32, (CHUNK, CHUNK), 1)
    keep = (col <= row, col >= row)
    r2 = lax.broadcasted_iota(jnp.int32, (2 * CHUNK, CHUNK), 0)
    c2 = lax.broadcasted_iota(jnp.int32, (2 * CHUNK, CHUNK), 1)
    one_if = lambda cond: jnp.where(cond, 1.0, 0.0)
    tri2 = jnp.where(r2 < CHUNK, one_if(c2 <= r2), one_if(c2 >= r2 - CHUNK)).astype(BF16)
    r2t = lax.broadcasted_iota(jnp.int32, (CHUNK, 2 * CHUNK), 0)
    c2t = lax.broadcasted_iota(jnp.int32, (CHUNK, 2 * CHUNK), 1)
    tri2t = jnp.where(c2t < CHUNK, one_if(r2t <= c2t), one_if(r2t >= c2t - CHUNK)).astype(BF16)
    sel_r = lax.broadcasted_iota(jnp.int32, (LANES, 2 * width), 0)
    sel_c = lax.broadcasted_iota(jnp.int32, (LANES, 2 * width), 1)
    head_lane = (sel_c // width) * heads + (sel_c % width) // SSD_HEAD_DIM
    sel2 = jnp.where(sel_r < 4 * heads, one_if(sel_r % (2 * heads) == head_lane), 0.0).astype(BF16)
    lane_c = lax.broadcasted_iota(jnp.int32, (CHUNK, LANES), 1)
    lane_x = lane_c
    neg_a2 = -jnp.exp(alog_ref[...]) * LOG2E

    def expand(src):
        lane = lax.broadcasted_iota(jnp.int32, src.shape, 1)
        hi = src.astype(BF16).astype(F32)
        packed = jnp.where(lane < 2 * heads, hi, pltpu.roll(src - hi, 2 * heads, 1))
        return jnp.dot(packed.astype(BF16), sel2, preferred_element_type=F32)

    def stage_a(c, carry):
        r0 = pl.multiple_of(c * CHUNK, CHUNK)
        xs = _conv_chunk(x_ref, wx_ref, bx_ref, c, n_chunks)
        bm_t = _conv_chunk(b_ref, wb_ref, bb_ref, c, n_chunks).T.astype(BF16)
        cm = _conv_chunk(c_ref, wc_ref, bc_ref, c, n_chunks).astype(BF16)
        cm_s[pl.ds(r0, CHUNK), :] = cm
        bmt_s[c] = bm_t

        dt = _softplus(dt_ref[pl.ds(r0, CHUNK), :] + dtb_ref[...])
        a2 = dt * neg_a2
        cs3 = jnp.dot(tri2, jnp.concatenate(_split3(a2), axis=1), preferred_element_type=F32)
        cs2 = cs3[:, :LANES] + cs3[:, LANES:2 * LANES] + cs3[:, 2 * LANES:]
        cst3 = jnp.dot(jnp.concatenate(_split3(a2.T), axis=0), tri2t, preferred_element_type=F32)
        cst2 = cst3[:CHUNK] + cst3[CHUNK:2 * CHUNK] + cst3[2 * CHUNK:]
        dt_t = dt.T

        is_fwd = lane_c < heads
        cs = jnp.where(is_fwd, cs2[:CHUNK], cs2[CHUNK:])
        total = jnp.where(is_fwd[:1], cs2[CHUNK - 1:CHUNK], cs2[CHUNK:CHUNK + 1])
        w = dt * jnp.exp2(total - cs)
        dec = jnp.broadcast_to(jnp.exp2(total), (dec_rows, LANES))
        ex = expand(jnp.concatenate([w, jnp.exp2(cs), dec], axis=0))
        for d in range(2):
            sl = slice(d * width, (d + 1) * width)
            xd_s[d, pl.ds(r0, CHUNK), :] = (xs * ex[:CHUNK, sl]).astype(BF16)
            ecs_s[d, pl.ds(r0, CHUNK), :] = ex[CHUNK:2 * CHUNK, sl]
            dec_s[d, pl.ds(pl.multiple_of(c * SUBLANES, SUBLANES), SUBLANES), :] = (
                ex[2 * CHUNK:2 * CHUNK + SUBLANES, sl])

        x_bf = xs.astype(BF16)
        cb = jnp.dot(cm, bm_t, preferred_element_type=F32)
        pieces = []
        for p in range(pairs):
            xp = x_bf[:, p * LANES:(p + 1) * LANES]
            zero = jnp.zeros_like(xp)
            x_bd = jnp.concatenate([jnp.where(lane_x < SSD_HEAD_DIM, xp, zero),
                                    jnp.where(lane_x >= SSD_HEAD_DIM, xp, zero)], axis=0)
            ms = []
            for d in range(2):
                for hh in (2 * p, 2 * p + 1):
                    j = d * heads + hh
                    diff = (cs2[d * CHUNK:(d + 1) * CHUNK, j:j + 1]
                            - cst2[j:j + 1, d * CHUNK:(d + 1) * CHUNK])
                    lm = jnp.exp2(jnp.where(keep[d], diff, -jnp.inf))
                    ms.append((cb * lm * dt_t[j:j + 1, :]).astype(BF16))
            pieces.append(jnp.dot(jnp.concatenate(ms, axis=1),
                                  jnp.concatenate([x_bd, x_bd], axis=0),
                                  preferred_element_type=F32))
        y_s[pl.ds(r0, CHUNK), :] = jnp.concatenate(pieces, axis=1) + dskip_ref[...] * xs
        return carry

    lax.fori_loop(0, n_chunks, stage_a, 0, unroll=2)

    st_s[...] = jnp.zeros_like(st_s)

    def stage_b(i, carry):
        for d, c in ((0, i), (1, n_chunks - 1 - i)):
            r0 = pl.multiple_of(c * CHUNK, CHUNK)
            state = st_s[d]
            y_off = jnp.dot(cm_s[pl.ds(r0, CHUNK), :], state.astype(BF16),
                            preferred_element_type=F32) * ecs_s[d, pl.ds(r0, CHUNK), :]
            y_s[pl.ds(r0, CHUNK), :] += y_off
            new = jnp.dot(bmt_s[c], xd_s[d, pl.ds(r0, CHUNK), :], preferred_element_type=F32)
            decay = dec_s[d, pl.ds(pl.multiple_of(c * SUBLANES, SUBLANES), SUBLANES), :][:1]
            st_s[d] = state * decay + new
        return carry

    lax.fori_loop(0, n_chunks, stage_b, 0)

    def stage_c(c, carry):
        r0 = pl.multiple_of(c * CHUNK, CHUNK)
        y = y_s[pl.ds(r0, CHUNK), :] * _silu(z_ref[pl.ds(r0, CHUNK), :])
        o_ref[pl.ds(r0, CHUNK), :] = _rms(y, gain_ref[...]).astype(BF16)
        return carry

    lax.fori_loop(0, n_chunks, stage_c, 0, unroll=2)


def _ssd(z, xbc, dt, conv_w, conv_b, dt_bias, a_log, d_skip, gain, *, batch, seq_len):
    T, d_ssd = z.shape
    width = d_ssd // N_GROUPS
    heads = width // SSD_HEAD_DIM
    n_chunks = seq_len // CHUNK
    b0 = d_ssd // D_STATE
    c0 = b0 + N_GROUPS

    def seq(cols, first):
        return pl.BlockSpec((seq_len, cols), lambda b, g: (b, first + g))

    def par(rows, cols, first):
        return pl.BlockSpec((rows, cols), lambda b, g: (0, first + g))

    return pl.pallas_call(
        functools.partial(_ssd_kernel, n_chunks=n_chunks, heads=heads),
        out_shape=jax.ShapeDtypeStruct((T, d_ssd), BF16),
        grid=(batch, N_GROUPS),
        in_specs=[
            seq(width, 0), seq(D_STATE, b0), seq(D_STATE, c0), seq(width, 0), seq(LANES, 0),
            par(D_CONV, width, 0), par(D_CONV, D_STATE, b0), par(D_CONV, D_STATE, c0),
            par(1, width, 0), par(1, D_STATE, b0), par(1, D_STATE, c0),
            par(1, LANES, 0), par(1, LANES, 0), par(1, width, 0), par(1, width, 0),
        ],
        out_specs=seq(width, 0),
        scratch_shapes=[
            pltpu.VMEM((seq_len, D_STATE), BF16),
            pltpu.VMEM((n_chunks, D_STATE, CHUNK), BF16),
            pltpu.VMEM((seq_len, width), F32),
            pltpu.VMEM((2, seq_len, width), BF16),
            pltpu.VMEM((2, seq_len, width), F32),
            pltpu.VMEM((2, n_chunks * SUBLANES, width), F32),
            pltpu.VMEM((2, D_STATE, width), F32),
        ],
        compiler_params=_cparams("parallel", "arbitrary"),
        name="ssd",
    )(xbc, xbc, xbc, z, dt, conv_w, conv_w, conv_w, conv_b, conv_b, conv_b,
      dt_bias, a_log, d_skip, gain)


def _out_kernel(a_ref, s_ref, h_ref, wa_ref, ws_ref, o_ref):
    o_ref[...] = (h_ref[...]
                  + jnp.dot(a_ref[...], wa_ref[...], preferred_element_type=F32)
                  + jnp.dot(s_ref[...], ws_ref[...], preferred_element_type=F32))


def _out_proj(a, s, h, w_out, *, layer, tm):
    T, D = h.shape
    da, ds = a.shape[1], s.shape[1]
    row = lambda i: (i, 0)
    return pl.pallas_call(
        _out_kernel,
        out_shape=jax.ShapeDtypeStruct((T, D), F32),
        grid=(T // tm,),
        in_specs=[
            pl.BlockSpec((tm, da), row),
            pl.BlockSpec((tm, ds), row),
            pl.BlockSpec((tm, D), row),
            pl.BlockSpec((None, da, D), lambda i: (layer, 0, 0)),
            pl.BlockSpec((None, ds, D), lambda i: (layer, da // ds, 0)),
        ],
        out_specs=pl.BlockSpec((tm, D), row),
        compiler_params=_cparams("parallel"),
        name="out_proj",
    )(a, s, h, w_out, w_out)


def _rope_tables(seq_len):
    axis_dim = HEAD_DIM // 2
    t = jnp.arange(seq_len)
    row_pos = (t // GRID_W).astype(F32)
    col_pos = (t % GRID_W).astype(F32)
    inv_freq = ROPE_THETA ** (-jnp.arange(0, axis_dim, 2, dtype=F32) / axis_dim)
    ang = jnp.concatenate([row_pos[:, None] * inv_freq, col_pos[:, None] * inv_freq], axis=-1)
    cos = jnp.concatenate([jnp.cos(ang), jnp.cos(ang)], axis=-1)
    sin = jnp.concatenate([-jnp.sin(ang), jnp.sin(ang)], axis=-1)
    return cos, sin


def _rope_layout(t):
    quarter = HEAD_DIM // 4
    lead = t.shape[:-1]
    t = t.reshape(lead + (-1, 2, 2, quarter))
    return jnp.swapaxes(t, -3, -2).reshape(lead + (-1,))


def _group_dt_columns(t, n_heads):
    hpg = n_heads // N_GROUPS
    lead = t.shape[:-1]
    t = t.reshape(lead + (2, N_GROUPS, hpg))
    t = jnp.moveaxis(t, -3, -2).reshape(lead + (N_GROUPS, 2 * hpg))
    t = jnp.pad(t, [(0, 0)] * (len(lead) + 1) + [(0, LANES - 2 * hpg)])
    return t.reshape(lead + (N_GROUPS * LANES,))


def kernel(x, ffn1_norm, ffn1_w_gu, ffn1_w_down, mix_norm, w_in, conv_w, conv_b, dt_bias, a_log, d_skip, q_norm, k_norm, attn_out_norm, ssd_out_norm, w_out, ffn2_norm, ffn2_w_gu, ffn2_w_down, final_norm):
    batch, seq_len, d_model = x.shape
    depth = ffn1_norm.shape[0]
    d_attn = attn_out_norm.shape[1]
    d_ssd = ssd_out_norm.shape[1]
    n_ssd_heads = d_skip.shape[1]
    n_q = d_attn // HEAD_DIM
    d_kv = N_KV_HEADS * HEAD_DIM
    d_xbc = conv_w.shape[2]
    d_qkv = d_attn + 2 * d_kv
    d_dt = N_GROUPS * LANES

    T = batch * seq_len
    tm = _token_tile(seq_len)
    tm_ffn, tf = _ffn_tiles(T, ffn1_w_down.shape[1])
    h = x.reshape(T, d_model)
    cos, sin = _rope_tables(seq_len)
    row = lambda v: v.reshape(1, -1)
    final_gain = row(final_norm)

    w_gu1, w_dn1 = ffn1_w_gu.astype(BF16), ffn1_w_down.astype(BF16)
    w_gu2, w_dn2 = ffn2_w_gu.astype(BF16), ffn2_w_down.astype(BF16)
    d_qk = d_attn + d_kv
    w_qkv = jnp.concatenate([_rope_layout(w_in[:, :, :d_qk]), w_in[:, :, d_qk:d_qkv]], axis=-1).astype(BF16)
    q_gain, k_gain = _rope_layout(q_norm), _rope_layout(k_norm)
    w_ssd = jnp.concatenate(
        [w_in[:, :, d_qkv:d_qkv + d_ssd + d_xbc],
         _group_dt_columns(w_in[:, :, d_qkv + d_ssd + d_xbc:], n_ssd_heads)], axis=-1).astype(BF16)
    w_o = w_out.astype(BF16)
    dt_bias_g = _group_dt_columns(dt_bias.reshape(depth, -1), n_ssd_heads)
    a_log_g = _group_dt_columns(a_log.reshape(depth, -1), n_ssd_heads)
    d_skip_e = jnp.repeat(d_skip, SSD_HEAD_DIM, axis=-1)

    for i in range(depth):
        last = i == depth - 1
        h = _ffn(h, row(ffn1_norm[i]), w_gu1, w_dn1, final_gain,
                 layer=i, tm=tm_ffn, tf=tf, final_norm=False)

        gain = row(mix_norm[i])
        q, k, v = _qkv_proj(h, gain, w_qkv, cos, sin, row(q_gain[i]), row(k_gain[i]),
                            layer=i, tm=tm, seq_len=seq_len, n_q=n_q, n_kv=N_KV_HEADS)
        z, xbc, dt = _ssd_proj(h, gain, w_ssd, layer=i, tm=tm, d_z=d_ssd, d_xbc=d_xbc, d_dt=d_dt)
        a_out = _attention(q, k, v, row(attn_out_norm[i]), batch=batch, seq_len=seq_len,
                           tq=tm, n_q=n_q, n_kv=N_KV_HEADS)
        s_out = _ssd(z, xbc, dt, conv_w[i], row(conv_b[i]), row(dt_bias_g[i]), row(a_log_g[i]),
                     row(d_skip_e[i]), row(ssd_out_norm[i]), batch=batch, seq_len=seq_len)
        h = _out_proj(a_out, s_out, h, w_o, layer=i, tm=tm)

        h = _ffn(h, row(ffn2_norm[i]), w_gu2, w_dn2, final_gain,
                 layer=i, tm=tm_ffn, tf=tf, final_norm=last)
    return h.reshape(batch, seq_len, d_model)
```

```python
import functools
import math

import jax
import jax.numpy as jnp
from jax import lax
from jax.experimental import pallas as pl
from jax.experimental.pallas import tpu as pltpu

F32 = jnp.float32
BF16 = jnp.bfloat16

EPS = 1e-6
GRID_W = 64
ROPE_THETA = 10000.0
HEAD_DIM = 128
N_KV_HEADS = 2
SSD_HEAD_DIM = 64
N_GROUPS = 2
D_STATE = 128
D_CONV = 5
CHUNK = 128
LANES = 128
SUBLANES = 8
MXU_WIDTH = 256
LOG2E = math.log2(math.e)
VMEM_LIMIT = 56 * 1024 * 1024


def _cparams(*sem):
    return pltpu.CompilerParams(dimension_semantics=sem, vmem_limit_bytes=VMEM_LIMIT)


def _token_tile(seq_len):
    return min(512, seq_len)


def _ffn_tiles(n_tokens, d_ff):
    tm = 1024 if n_tokens % 1024 == 0 else 512
    tf = 512 if d_ff % 512 == 0 else LANES
    return tm, tf


def _rms(x, gain):
    r = lax.rsqrt(jnp.mean(x * x, axis=-1, keepdims=True) + EPS)
    return x * r * gain


def _ffn_kernel(x_hbm, gain_ref, wg_ref, wu_ref, wd_ref, fgain_ref, o_ref, x_s, xn_ref, sem,
                *, tm, prefetch_step, final_norm):
    i = pl.program_id(0)
    j = pl.program_id(1)

    def x_copy(tile):
        return pltpu.make_async_copy(x_hbm.at[pl.ds(tile * tm, tm), :], x_s, sem)

    @pl.when(j == 0)
    def _():
        @pl.when(i == 0)
        def _():
            x_copy(0).start()

        x_copy(i).wait()
        x = x_s[...]
        xn_ref[...] = _rms(x, gain_ref[...]).astype(BF16)
        o_ref[...] = x

    @pl.when((j == prefetch_step) & (i + 1 < pl.num_programs(0)))
    def _():
        x_copy(i + 1).start()

    xn = xn_ref[...]
    g = jnp.dot(xn, wg_ref[...], preferred_element_type=F32)
    u = jnp.dot(xn, wu_ref[...], preferred_element_type=F32)
    h = (g * jax.nn.sigmoid(g) * (0.5 * u)).astype(BF16)
    o_ref[...] += jnp.dot(h, wd_ref[...], preferred_element_type=F32)

    if final_norm:
        @pl.when(j == pl.num_programs(1) - 1)
        def _():
            o_ref[...] = _rms(o_ref[...], fgain_ref[...])


def _ffn(h, gain, w_gu, w_down, final_gain, *, layer, tm, tf, final_norm):
    T, D = h.shape
    FF = w_down.shape[1]
    nj = FF // tf
    return pl.pallas_call(
        functools.partial(_ffn_kernel, tm=tm, prefetch_step=min(2, nj - 1), final_norm=final_norm),
        out_shape=jax.ShapeDtypeStruct((T, D), F32),
        grid=(T // tm, nj),
        in_specs=[
            pl.BlockSpec(memory_space=pl.ANY),
            pl.BlockSpec((1, D), lambda i, j: (0, 0)),
            pl.BlockSpec((None, D, tf), lambda i, j: (layer, 0, j)),
            pl.BlockSpec((None, D, tf), lambda i, j: (layer, 0, j + nj)),
            pl.BlockSpec((None, tf, D), lambda i, j: (layer, j, 0)),
            pl.BlockSpec((1, D), lambda i, j: (0, 0)),
        ],
        out_specs=pl.BlockSpec((tm, D), lambda i, j: (i, 0)),
        scratch_shapes=[pltpu.VMEM((tm, D), F32), pltpu.VMEM((tm, D), BF16), pltpu.SemaphoreType.DMA(())],
        compiler_params=_cparams("arbitrary", "arbitrary"),
        name="ffn_final" if final_norm else "ffn",
    )(h, gain, w_gu, w_gu, w_down, final_gain)


def _rope(x, cos, sin_signed):
    return x * cos + pltpu.roll(x, HEAD_DIM // 2, 1) * sin_signed


def _qkv_kernel(h_ref, gain_ref, w_ref, cos_ref, sin_ref, qg_ref, kg_ref,
                q_ref, k_ref, v_ref, *, n_q, n_kv, q_scale):
    xn = _rms(h_ref[...], gain_ref[...]).astype(BF16)
    cos = cos_ref[...]
    sin = sin_ref[...]
    heads_per_dot = MXU_WIDTH // HEAD_DIM
    n_heads = n_q + 2 * n_kv
    for g in range(n_heads // heads_per_dot):
        c0 = g * MXU_WIDTH
        p = jnp.dot(xn, w_ref[:, c0:c0 + MXU_WIDTH], preferred_element_type=F32)
        for sub in range(heads_per_dot):
            hd = g * heads_per_dot + sub
            ph = p[:, sub * HEAD_DIM:(sub + 1) * HEAD_DIM]
            if hd < n_q:
                qh = _rope(_rms(ph, qg_ref[...]), cos, sin) * q_scale
                q_ref[:, hd * HEAD_DIM:(hd + 1) * HEAD_DIM] = qh.astype(BF16)
            elif hd < n_q + n_kv:
                kh = _rope(_rms(ph, kg_ref[...]), cos, sin)
                kk = hd - n_q
                k_ref[:, kk * HEAD_DIM:(kk + 1) * HEAD_DIM] = kh.astype(BF16)
            else:
                vv = hd - n_q - n_kv
                v_ref[:, vv * HEAD_DIM:(vv + 1) * HEAD_DIM] = ph.astype(BF16)


def _qkv_proj(h, gain, w_qkv, cos, sin, q_gain, k_gain, *, layer, tm, seq_len, n_q, n_kv):
    T, D = h.shape
    tiles_per_seq = seq_len // tm
    dq, dkv = n_q * HEAD_DIM, n_kv * HEAD_DIM
    row = lambda i: (i, 0)
    const = lambda i: (0, 0)
    pos = lambda i: (i % tiles_per_seq, 0)
    return pl.pallas_call(
        functools.partial(_qkv_kernel, n_q=n_q, n_kv=n_kv, q_scale=HEAD_DIM ** -0.5 * LOG2E),
        out_shape=(jax.ShapeDtypeStruct((T, dq), BF16),
                   jax.ShapeDtypeStruct((T, dkv), BF16),
                   jax.ShapeDtypeStruct((T, dkv), BF16)),
        grid=(T // tm,),
        in_specs=[
            pl.BlockSpec((tm, D), row),
            pl.BlockSpec((1, D), const),
            pl.BlockSpec((None, D, dq + 2 * dkv), lambda i: (layer, 0, 0)),
            pl.BlockSpec((tm, HEAD_DIM), pos),
            pl.BlockSpec((tm, HEAD_DIM), pos),
            pl.BlockSpec((1, HEAD_DIM), const),
            pl.BlockSpec((1, HEAD_DIM), const),
        ],
        out_specs=(pl.BlockSpec((tm, dq), row),
                   pl.BlockSpec((tm, dkv), row),
                   pl.BlockSpec((tm, dkv), row)),
        compiler_params=_cparams("parallel"),
        name="qkv_proj",
    )(h, gain, w_qkv, cos, sin, q_gain, k_gain)


def _silu(x):
    h = 0.5 * x
    return h + h * jnp.tanh(h)


def _ssd_proj_kernel(h_ref, hp_ref, hn_ref, gain_ref, w_ref, cw_ref, cb_ref,
                     xs_ref, bm_ref, cm_ref, z_ref, dt_ref, *, d_x, d_bc, d_z, tiles_per_seq, col_tile):
    tm = h_ref.shape[0]
    halo = hp_ref.shape[0]
    pos = pl.program_id(0) % tiles_per_seq
    gain = gain_ref[...]
    has_prev = jnp.where(pos > 0, 1.0, 0.0)
    has_next = jnp.where(pos < tiles_per_seq - 1, 1.0, 0.0)
    xn = _rms(h_ref[...], gain).astype(BF16)
    xn_ext = jnp.concatenate([(_rms(hp_ref[...], gain) * has_prev).astype(BF16), xn,
                              (_rms(hn_ref[...], gain) * has_next).astype(BF16)], axis=0)
    rows = tm + 2 * halo
    pad = (D_CONV - 1) // 2

    def conv_cols(c0, c1):
        p_all = jnp.dot(xn_ext, w_ref[:, c0:c1], preferred_element_type=F32)
        outs = []
        for l0 in range(0, c1 - c0, LANES):
            p = p_all[:, l0:l0 + LANES]
            acc = None
            for tap in range(D_CONV):
                shift = pad - tap
                shifted = p if shift == 0 else pltpu.roll(p, shift % rows, 0)
                term = shifted[halo:halo + tm] * cw_ref[tap:tap + 1, c0 + l0:c0 + l0 + LANES]
                acc = term if acc is None else acc + term
            outs.append(_silu(acc + cb_ref[:, c0 + l0:c0 + l0 + LANES]))
        return jnp.concatenate(outs, axis=1)

    for c0 in range(0, d_x, col_tile):
        xs_ref[:, c0:c0 + col_tile] = conv_cols(c0, c0 + col_tile)
    bc = conv_cols(d_x, d_x + 2 * d_bc).astype(BF16)
    bm_ref[...] = bc[:, :d_bc]
    cm_ref[...] = bc[:, d_bc:]
    z0 = d_x + 2 * d_bc
    z_ref[...] = jnp.dot(xn, w_ref[:, z0:z0 + d_z], preferred_element_type=F32)
    dt_ref[...] = jnp.dot(xn, w_ref[:, z0 + d_z:], preferred_element_type=F32)


def _ssd_proj(h, gain, w_ssd, conv_w, conv_b, *, layer, tm, seq_len, d_x, d_bc, d_z, d_dt):
    T, D = h.shape
    halo = 2 * SUBLANES
    per_tile = tm // halo
    n_halo_blocks = T // halo
    row = lambda i: (i, 0)
    const = lambda i: (0, 0)
    return pl.pallas_call(
        functools.partial(_ssd_proj_kernel, d_x=d_x, d_bc=d_bc, d_z=d_z,
                          tiles_per_seq=seq_len // tm, col_tile=2 * MXU_WIDTH),
        out_shape=(jax.ShapeDtypeStruct((T, d_x), F32),
                   jax.ShapeDtypeStruct((T, d_bc), BF16),
                   jax.ShapeDtypeStruct((T, d_bc), BF16),
                   jax.ShapeDtypeStruct((T, d_z), F32),
                   jax.ShapeDtypeStruct((T, d_dt), F32)),
        grid=(T // tm,),
        in_specs=[
            pl.BlockSpec((tm, D), row),
            pl.BlockSpec((halo, D), lambda i: (jnp.maximum(i * per_tile - 1, 0), 0)),
            pl.BlockSpec((halo, D), lambda i: (jnp.minimum((i + 1) * per_tile, n_halo_blocks - 1), 0)),
            pl.BlockSpec((1, D), const),
            pl.BlockSpec((None, D, d_x + 2 * d_bc + d_z + d_dt), lambda i: (layer, 0, 0)),
            pl.BlockSpec((D_CONV, d_x + 2 * d_bc), const),
            pl.BlockSpec((1, d_x + 2 * d_bc), const),
        ],
        out_specs=(pl.BlockSpec((tm, d_x), row),
                   pl.BlockSpec((tm, d_bc), row),
                   pl.BlockSpec((tm, d_bc), row),
                   pl.BlockSpec((tm, d_z), row),
                   pl.BlockSpec((tm, d_dt), row)),
        compiler_params=_cparams("parallel"),
        name="ssd_proj",
    )(h, h, h, gain, w_ssd, conv_w, conv_b)


def _attn_kernel(q_ref, k_ref, v_ref, gain_ref, o_ref, acc_ref, *, n_q, n_kv):
    q_per_kv = n_q // n_kv
    for hd in range(n_q):
        kv = hd // q_per_kv
        qh = q_ref[:, hd * HEAD_DIM:(hd + 1) * HEAD_DIM]
        kh = k_ref[:, kv * HEAD_DIM:(kv + 1) * HEAD_DIM]
        vh = v_ref[:, kv * HEAD_DIM:(kv + 1) * HEAD_DIM]
        s = lax.dot_general(qh, kh, (((1,), (1,)), ((), ())), preferred_element_type=F32)
        m = jnp.max(s, axis=-1, keepdims=True)
        p = jnp.exp2(s - m)
        l = jnp.sum(p, axis=-1, keepdims=True)
        o = jnp.dot(p.astype(BF16), vh, preferred_element_type=F32)
        acc_ref[:, hd * HEAD_DIM:(hd + 1) * HEAD_DIM] = o * (1.0 / l)
    o_ref[...] = _rms(acc_ref[...], gain_ref[...]).astype(BF16)


def _attention(q, k, v, gain, *, batch, seq_len, tq, n_q, n_kv):
    T, dq = q.shape
    dkv = k.shape[1]
    nq_tiles = seq_len // tq
    return pl.pallas_call(
        functools.partial(_attn_kernel, n_q=n_q, n_kv=n_kv),
        out_shape=jax.ShapeDtypeStruct((T, dq), BF16),
        grid=(batch, nq_tiles),
        in_specs=[
            pl.BlockSpec((tq, dq), lambda b, i: (b * nq_tiles + i, 0)),
            pl.BlockSpec((seq_len, dkv), lambda b, i: (b, 0)),
            pl.BlockSpec((seq_len, dkv), lambda b, i: (b, 0)),
            pl.BlockSpec((1, dq), lambda b, i: (0, 0)),
        ],
        out_specs=pl.BlockSpec((tq, dq), lambda b, i: (b * nq_tiles + i, 0)),
        scratch_shapes=[pltpu.VMEM((tq, dq), F32)],
        compiler_params=_cparams("parallel", "arbitrary"),
        name="attention",
    )(q, k, v, gain)


def _split3(a):
    hi = a.astype(BF16)
    r1 = a - hi.astype(F32)
    mid = r1.astype(BF16)
    lo = (r1 - mid.astype(F32)).astype(BF16)
    return hi, mid, lo


def _softplus(x):
    return jnp.maximum(x, 0.0) + jnp.log1p(jnp.exp(-jnp.abs(x)))


def _ssd_kernel(x_ref, b_ref, c_ref, z_ref, dt_ref, dtb_ref, alog_ref, dskip_ref, gain_ref,
                o_ref,
                bmt_s, y_s, xd_s, ecs_s, dec_s, st_s, *, n_chunks, heads):
    width = heads * SSD_HEAD_DIM
    pairs = width // LANES
    dec_rows = 2 * SUBLANES

    row = lax.broadcasted_iota(jnp.int32, (CHUNK, CHUNK), 0)
    col = lax.broadcasted_iota(jnp.int32, (CHUNK, CHUNK), 1)
    keep = (col <= row, col >= row)
    r2 = lax.broadcasted_iota(jnp.int32, (2 * CHUNK, CHUNK), 0)
    c2 = lax.broadcasted_iota(jnp.int32, (2 * CHUNK, CHUNK), 1)
    one_if = lambda cond: jnp.where(cond, 1.0, 0.0)
    tri2 = jnp.where(r2 < CHUNK, one_if(c2 <= r2), one_if(c2 >= r2 - CHUNK)).astype(BF16)
    r2t = lax.broadcasted_iota(jnp.int32, (CHUNK, 2 * CHUNK), 0)
    c2t = lax.broadcasted_iota(jnp.int32, (CHUNK, 2 * CHUNK), 1)
    tri2t = jnp.where(c2t < CHUNK, one_if(r2t <= c2t), one_if(r2t >= c2t - CHUNK)).astype(BF16)
    sel_r = lax.broadcasted_iota(jnp.int32, (LANES, 2 * width), 0)
    sel_c = lax.broadcasted_iota(jnp.int32, (LANES, 2 * width), 1)
    head_lane = (sel_c // width) * heads + (sel_c % width) // SSD_HEAD_DIM
    sel2 = jnp.where(sel_r < 4 * heads, one_if(sel_r % (2 * heads) == head_lane), 0.0).astype(BF16)
    lane_c = lax.broadcasted_iota(jnp.int32, (CHUNK, LANES), 1)
    lane_x = lane_c
    neg_a2 = -jnp.exp(alog_ref[...]) * LOG2E

    def expand(src):
        lane = lax.broadcasted_iota(jnp.int32, src.shape, 1)
        hi = src.astype(BF16).astype(F32)
        packed = jnp.where(lane < 2 * heads, hi, pltpu.roll(src - hi, 2 * heads, 1))
        return jnp.dot(packed.astype(BF16), sel2, preferred_element_type=F32)

    def stage_a(c, carry):
        r0 = pl.multiple_of(c * CHUNK, CHUNK)
        xs = x_ref[pl.ds(r0, CHUNK), :]
        cm = c_ref[pl.ds(r0, CHUNK), :]
        bm_t = b_ref[pl.ds(r0, CHUNK), :].astype(F32).T.astype(BF16)
        bmt_s[c] = bm_t

        dt = _softplus(dt_ref[pl.ds(r0, CHUNK), :] + dtb_ref[...])
        a2 = dt * neg_a2
        cs3 = jnp.dot(tri2, jnp.concatenate(_split3(a2), axis=1), preferred_element_type=F32)
        cs2 = cs3[:, :LANES] + cs3[:, LANES:2 * LANES] + cs3[:, 2 * LANES:]
        cst3 = jnp.dot(jnp.concatenate(_split3(a2.T), axis=0), tri2t, preferred_element_type=F32)
        cst2 = cst3[:CHUNK] + cst3[CHUNK:2 * CHUNK] + cst3[2 * CHUNK:]
        dt_t = dt.T

        is_fwd = lane_c < heads
        cs = jnp.where(is_fwd, cs2[:CHUNK], cs2[CHUNK:])
        total = jnp.where(is_fwd[:1], cs2[CHUNK - 1:CHUNK], cs2[CHUNK:CHUNK + 1])
        w = dt * jnp.exp2(total - cs)
        dec = jnp.broadcast_to(jnp.exp2(total), (dec_rows, LANES))
        ex = expand(jnp.concatenate([w, jnp.exp2(cs), dec], axis=0))
        for d in range(2):
            sl = slice(d * width, (d + 1) * width)
            xd_s[d, pl.ds(r0, CHUNK), :] = (xs * ex[:CHUNK, sl]).astype(BF16)
            ecs_s[d, pl.ds(r0, CHUNK), :] = ex[CHUNK:2 * CHUNK, sl]
            dec_s[d, pl.ds(pl.multiple_of(c * SUBLANES, SUBLANES), SUBLANES), :] = (
                ex[2 * CHUNK:2 * CHUNK + SUBLANES, sl])

        x_bf = xs.astype(BF16)
        cb = jnp.dot(cm, bm_t, preferred_element_type=F32)
        pieces = []
        for p in range(pairs):
            xp = x_bf[:, p * LANES:(p + 1) * LANES]
            zero = jnp.zeros_like(xp)
            x_bd = jnp.concatenate([jnp.where(lane_x < SSD_HEAD_DIM, xp, zero),
                                    jnp.where(lane_x >= SSD_HEAD_DIM, xp, zero)], axis=0)
            ms = []
            for d in range(2):
                for hh in (2 * p, 2 * p + 1):
                    j = d * heads + hh
                    diff = (cs2[d * CHUNK:(d + 1) * CHUNK, j:j + 1]
                            - cst2[j:j + 1, d * CHUNK:(d + 1) * CHUNK])
                    lm = jnp.exp2(jnp.where(keep[d], diff, -jnp.inf))
                    ms.append((cb * lm * dt_t[j:j + 1, :]).astype(BF16))
            pieces.append(jnp.dot(jnp.concatenate(ms, axis=1),
                                  jnp.concatenate([x_bd, x_bd], axis=0),
                                  preferred_element_type=F32))
        y_s[pl.ds(r0, CHUNK), :] = jnp.concatenate(pieces, axis=1) + dskip_ref[...] * xs
        return carry

    lax.fori_loop(0, n_chunks, stage_a, 0, unroll=4)

    st_s[...] = jnp.zeros_like(st_s)

    def stage_b(i, carry):
        for d, c in ((0, i), (1, n_chunks - 1 - i)):
            r0 = pl.multiple_of(c * CHUNK, CHUNK)
            state = st_s[d]
            y_off = jnp.dot(c_ref[pl.ds(r0, CHUNK), :], state.astype(BF16),
                            preferred_element_type=F32) * ecs_s[d, pl.ds(r0, CHUNK), :]
            y_s[pl.ds(r0, CHUNK), :] += y_off
            new = jnp.dot(bmt_s[c], xd_s[d, pl.ds(r0, CHUNK), :], preferred_element_type=F32)
            decay = dec_s[d, pl.ds(pl.multiple_of(c * SUBLANES, SUBLANES), SUBLANES), :][:1]
            st_s[d] = state * decay + new
        return carry

    lax.fori_loop(0, n_chunks, stage_b, 0, unroll=2)

    def stage_c(c, carry):
        r0 = pl.multiple_of(c * CHUNK, CHUNK)
        y = y_s[pl.ds(r0, CHUNK), :] * _silu(z_ref[pl.ds(r0, CHUNK), :])
        o_ref[pl.ds(r0, CHUNK), :] = _rms(y, gain_ref[...]).astype(BF16)
        return carry

    lax.fori_loop(0, n_chunks, stage_c, 0, unroll=2)


def _ssd(xs, bm, cm, z, dt, dt_bias, a_log, d_skip, gain, *, batch, seq_len):
    T, d_ssd = z.shape
    width = d_ssd // N_GROUPS
    heads = width // SSD_HEAD_DIM
    n_chunks = seq_len // CHUNK

    def seq(cols):
        return pl.BlockSpec((seq_len, cols), lambda b, g: (b, g))

    def par(cols):
        return pl.BlockSpec((1, cols), lambda b, g: (0, g))

    return pl.pallas_call(
        functools.partial(_ssd_kernel, n_chunks=n_chunks, heads=heads),
        out_shape=jax.ShapeDtypeStruct((T, d_ssd), BF16),
        grid=(batch, N_GROUPS),
        in_specs=[
            seq(width), seq(D_STATE), seq(D_STATE), seq(width), seq(LANES),
            par(LANES), par(LANES), par(width), par(width),
        ],
        out_specs=seq(width),
        scratch_shapes=[
            pltpu.VMEM((n_chunks, D_STATE, CHUNK), BF16),
            pltpu.VMEM((seq_len, width), F32),
            pltpu.VMEM((2, seq_len, width), BF16),
            pltpu.VMEM((2, seq_len, width), F32),
            pltpu.VMEM((2, n_chunks * SUBLANES, width), F32),
            pltpu.VMEM((2, D_STATE, width), F32),
        ],
        compiler_params=_cparams("parallel", "arbitrary"),
        name="ssd",
    )(xs, bm, cm, z, dt, dt_bias, a_log, d_skip, gain)


def _out_kernel(a_ref, s_ref, h_ref, wa_ref, ws_ref, o_ref):
    o_ref[...] = (h_ref[...]
                  + jnp.dot(a_ref[...], wa_ref[...], preferred_element_type=F32)
                  + jnp.dot(s_ref[...], ws_ref[...], preferred_element_type=F32))


def _out_proj(a, s, h, w_out, *, layer, tm):
    T, D = h.shape
    da, ds = a.shape[1], s.shape[1]
    row = lambda i: (i, 0)
    return pl.pallas_call(
        _out_kernel,
        out_shape=jax.ShapeDtypeStruct((T, D), F32),
        grid=(T // tm,),
        in_specs=[
            pl.BlockSpec((tm, da), row),
            pl.BlockSpec((tm, ds), row),
            pl.BlockSpec((tm, D), row),
            pl.BlockSpec((None, da, D), lambda i: (layer, 0, 0)),
            pl.BlockSpec((None, ds, D), lambda i: (layer, da // ds, 0)),
        ],
        out_specs=pl.BlockSpec((tm, D), row),
        compiler_params=_cparams("parallel"),
        name="out_proj",
    )(a, s, h, w_out, w_out)


def _rope_tables(seq_len):
    axis_dim = HEAD_DIM // 2
    t = jnp.arange(seq_len)
    row_pos = (t // GRID_W).astype(F32)
    col_pos = (t % GRID_W).astype(F32)
    inv_freq = ROPE_THETA ** (-jnp.arange(0, axis_dim, 2, dtype=F32) / axis_dim)
    ang = jnp.concatenate([row_pos[:, None] * inv_freq, col_pos[:, None] * inv_freq], axis=-1)
    cos = jnp.concatenate([jnp.cos(ang), jnp.cos(ang)], axis=-1)
    sin = jnp.concatenate([-jnp.sin(ang), jnp.sin(ang)], axis=-1)
    return cos, sin


def _rope_layout(t):
    quarter = HEAD_DIM // 4
    lead = t.shape[:-1]
    t = t.reshape(lead + (-1, 2, 2, quarter))
    return jnp.swapaxes(t, -3, -2).reshape(lead + (-1,))


def _group_dt_columns(t, n_heads):
    hpg = n_heads // N_GROUPS
    lead = t.shape[:-1]
    t = t.reshape(lead + (2, N_GROUPS, hpg))
    t = jnp.moveaxis(t, -3, -2).reshape(lead + (N_GROUPS, 2 * hpg))
    t = jnp.pad(t, [(0, 0)] * (len(lead) + 1) + [(0, LANES - 2 * hpg)])
    return t.reshape(lead + (N_GROUPS * LANES,))


def kernel(x, ffn1_norm, ffn1_w_gu, ffn1_w_down, mix_norm, w_in, conv_w, conv_b, dt_bias, a_log, d_skip, q_norm, k_norm, attn_out_norm, ssd_out_norm, w_out, ffn2_norm, ffn2_w_gu, ffn2_w_down, final_norm):
    batch, seq_len, d_model = x.shape
    depth = ffn1_norm.shape[0]
    d_attn = attn_out_norm.shape[1]
    d_ssd = ssd_out_norm.shape[1]
    n_ssd_heads = d_skip.shape[1]
    n_q = d_attn // HEAD_DIM
    d_kv = N_KV_HEADS * HEAD_DIM
    d_xbc = conv_w.shape[2]
    d_qkv = d_attn + 2 * d_kv
    d_dt = N_GROUPS * LANES

    T = batch * seq_len
    tm = _token_tile(seq_len)
    tm_ffn, tf = _ffn_tiles(T, ffn1_w_down.shape[1])
    h = x.reshape(T, d_model)
    cos, sin = _rope_tables(seq_len)
    row = lambda v: v.reshape(1, -1)
    final_gain = row(final_norm)

    w_gu1, w_dn1 = ffn1_w_gu.astype(BF16), ffn1_w_down.astype(BF16)
    w_gu2, w_dn2 = ffn2_w_gu.astype(BF16), ffn2_w_down.astype(BF16)
    d_qk = d_attn + d_kv
    w_qkv = jnp.concatenate([_rope_layout(w_in[:, :, :d_qk]), w_in[:, :, d_qk:d_qkv]], axis=-1).astype(BF16)
    q_gain, k_gain = _rope_layout(q_norm), _rope_layout(k_norm)
    z0 = d_qkv
    xbc0 = z0 + d_ssd
    dt0 = xbc0 + d_xbc
    w_ssd = jnp.concatenate([w_in[:, :, xbc0:dt0], w_in[:, :, z0:xbc0],
                             _group_dt_columns(w_in[:, :, dt0:], n_ssd_heads)], axis=-1).astype(BF16)
    w_o = w_out.astype(BF16)
    dt_bias_g = _group_dt_columns(dt_bias.reshape(depth, -1), n_ssd_heads)
    a_log_g = _group_dt_columns(a_log.reshape(depth, -1), n_ssd_heads)
    d_skip_e = jnp.repeat(d_skip, SSD_HEAD_DIM, axis=-1)

    for i in range(depth):
        last = i == depth - 1
        h = _ffn(h, row(ffn1_norm[i]), w_gu1, w_dn1, final_gain,
                 layer=i, tm=tm_ffn, tf=tf, final_norm=False)

        gain = row(mix_norm[i])
        q, k, v = _qkv_proj(h, gain, w_qkv, cos, sin, row(q_gain[i]), row(k_gain[i]),
                            layer=i, tm=tm, seq_len=seq_len, n_q=n_q, n_kv=N_KV_HEADS)
        xs, bm, cm, z, dt = _ssd_proj(h, gain, w_ssd, conv_w[i], row(conv_b[i]), layer=i, tm=tm,
                                      seq_len=seq_len, d_x=d_ssd, d_bc=(d_xbc - d_ssd) // 2,
                                      d_z=d_ssd, d_dt=d_dt)
        a_out = _attention(q, k, v, row(attn_out_norm[i]), batch=batch, seq_len=seq_len,
                           tq=tm, n_q=n_q, n_kv=N_KV_HEADS)
        s_out = _ssd(xs, bm, cm, z, dt, row(dt_bias_g[i]), row(a_log_g[i]),
                     row(d_skip_e[i]), row(ssd_out_norm[i]), batch=batch, seq_len=seq_len)
        h = _out_proj(a_out, s_out, h, w_o, layer=i, tm=tm)

        h = _ffn(h, row(ffn2_norm[i]), w_gu2, w_dn2, final_gain,
                 layer=i, tm=tm_ffn, tf=tf, final_norm=last)
    return h.reshape(batch, seq_len, d_model)
```

```python
import functools
import math

import jax
import jax.numpy as jnp
from jax import lax
from jax.experimental import pallas as pl
from jax.experimental.pallas import tpu as pltpu

F32 = jnp.float32
BF16 = jnp.bfloat16

EPS = 1e-6
GRID_W = 64
ROPE_THETA = 10000.0
HEAD_DIM = 128
N_KV_HEADS = 2
SSD_HEAD_DIM = 64
N_GROUPS = 2
D_STATE = 128
D_CONV = 5
CHUNK = 128
LANES = 128
SUBLANES = 8
MXU_WIDTH = 256
LOG2E = math.log2(math.e)
VMEM_LIMIT = 56 * 1024 * 1024


def _cparams(*sem):
    return pltpu.CompilerParams(dimension_semantics=sem, vmem_limit_bytes=VMEM_LIMIT)


def _token_tile(seq_len):
    return min(512, seq_len)


def _ffn_tiles(n_tokens, d_ff):
    tm = 1024 if n_tokens % 1024 == 0 else 512
    tf = 512 if d_ff % 512 == 0 else LANES
    return tm, tf


def _rms(x, gain):
    r = lax.rsqrt(jnp.mean(x * x, axis=-1, keepdims=True) + EPS)
    return x * r * gain


def _ffn_kernel(x_hbm, gain_ref, wg_ref, wu_ref, wd_ref, fgain_ref, o_ref, x_s, xn_ref, sem,
                *, tm, prefetch_step, final_norm):
    i = pl.program_id(0)
    j = pl.program_id(1)

    def x_copy(tile):
        return pltpu.make_async_copy(x_hbm.at[pl.ds(tile * tm, tm), :], x_s, sem)

    @pl.when(j == 0)
    def _():
        @pl.when(i == 0)
        def _():
            x_copy(0).start()

        x_copy(i).wait()
        x = x_s[...]
        xn_ref[...] = _rms(x, gain_ref[...]).astype(BF16)
        o_ref[...] = x

    @pl.when((j == prefetch_step) & (i + 1 < pl.num_programs(0)))
    def _():
        x_copy(i + 1).start()

    xn = xn_ref[...]
    g = jnp.dot(xn, wg_ref[...], preferred_element_type=F32)
    u = jnp.dot(xn, wu_ref[...], preferred_element_type=F32)
    h = (g * jax.nn.sigmoid(g) * (0.5 * u)).astype(BF16)
    o_ref[...] += jnp.dot(h, wd_ref[...], preferred_element_type=F32)

    if final_norm:
        @pl.when(j == pl.num_programs(1) - 1)
        def _():
            o_ref[...] = _rms(o_ref[...], fgain_ref[...])


def _ffn(h, gain, w_gu, w_down, final_gain, *, layer, tm, tf, final_norm):
    T, D = h.shape
    FF = w_down.shape[1]
    nj = FF // tf
    return pl.pallas_call(
        functools.partial(_ffn_kernel, tm=tm, prefetch_step=min(2, nj - 1), final_norm=final_norm),
        out_shape=jax.ShapeDtypeStruct((T, D), F32),
        grid=(T // tm, nj),
        in_specs=[
            pl.BlockSpec(memory_space=pl.ANY),
            pl.BlockSpec((1, D), lambda i, j: (0, 0)),
            pl.BlockSpec((None, D, tf), lambda i, j: (layer, 0, j)),
            pl.BlockSpec((None, D, tf), lambda i, j: (layer, 0, j + nj)),
            pl.BlockSpec((None, tf, D), lambda i, j: (layer, j, 0)),
            pl.BlockSpec((1, D), lambda i, j: (0, 0)),
        ],
        out_specs=pl.BlockSpec((tm, D), lambda i, j: (i, 0)),
        scratch_shapes=[pltpu.VMEM((tm, D), F32), pltpu.VMEM((tm, D), BF16), pltpu.SemaphoreType.DMA(())],
        compiler_params=_cparams("arbitrary", "arbitrary"),
        name="ffn_final" if final_norm else "ffn",
    )(h, gain, w_gu, w_gu, w_down, final_gain)


def _rope(x, cos, sin_signed):
    return x * cos + pltpu.roll(x, HEAD_DIM // 2, 1) * sin_signed


def _qkv_kernel(h_ref, gain_ref, w_ref, cos_ref, sin_ref, qg_ref, kg_ref,
                q_ref, k_ref, v_ref, *, n_q, n_kv, q_scale):
    xn = _rms(h_ref[...], gain_ref[...]).astype(BF16)
    cos = cos_ref[...]
    sin = sin_ref[...]
    heads_per_dot = MXU_WIDTH // HEAD_DIM
    n_heads = n_q + 2 * n_kv
    for g in range(n_heads // heads_per_dot):
        c0 = g * MXU_WIDTH
        p = jnp.dot(xn, w_ref[:, c0:c0 + MXU_WIDTH], preferred_element_type=F32)
        for sub in range(heads_per_dot):
            hd = g * heads_per_dot + sub
            ph = p[:, sub * HEAD_DIM:(sub + 1) * HEAD_DIM]
            if hd < n_q:
                qh = _rope(_rms(ph, qg_ref[...]), cos, sin) * q_scale
                q_ref[:, hd * HEAD_DIM:(hd + 1) * HEAD_DIM] = qh.astype(BF16)
            elif hd < n_q + n_kv:
                kh = _rope(_rms(ph, kg_ref[...]), cos, sin)
                kk = hd - n_q
                k_ref[:, kk * HEAD_DIM:(kk + 1) * HEAD_DIM] = kh.astype(BF16)
            else:
                vv = hd - n_q - n_kv
                v_ref[:, vv * HEAD_DIM:(vv + 1) * HEAD_DIM] = ph.astype(BF16)


def _qkv_proj(h, gain, w_qkv, cos, sin, q_gain, k_gain, *, layer, tm, seq_len, n_q, n_kv):
    T, D = h.shape
    tiles_per_seq = seq_len // tm
    dq, dkv = n_q * HEAD_DIM, n_kv * HEAD_DIM
    row = lambda i: (i, 0)
    const = lambda i: (0, 0)
    pos = lambda i: (i % tiles_per_seq, 0)
    return pl.pallas_call(
        functools.partial(_qkv_kernel, n_q=n_q, n_kv=n_kv, q_scale=HEAD_DIM ** -0.5 * LOG2E),
        out_shape=(jax.ShapeDtypeStruct((T, dq), BF16),
                   jax.ShapeDtypeStruct((T, dkv), BF16),
                   jax.ShapeDtypeStruct((T, dkv), BF16)),
        grid=(T // tm,),
        in_specs=[
            pl.BlockSpec((tm, D), row),
            pl.BlockSpec((1, D), const),
            pl.BlockSpec((None, D, dq + 2 * dkv), lambda i: (layer, 0, 0)),
            pl.BlockSpec((tm, HEAD_DIM), pos),
            pl.BlockSpec((tm, HEAD_DIM), pos),
            pl.BlockSpec((1, HEAD_DIM), const),
            pl.BlockSpec((1, HEAD_DIM), const),
        ],
        out_specs=(pl.BlockSpec((tm, dq), row),
                   pl.BlockSpec((tm, dkv), row),
                   pl.BlockSpec((tm, dkv), row)),
        compiler_params=_cparams("parallel"),
        name="qkv_proj",
    )(h, gain, w_qkv, cos, sin, q_gain, k_gain)


def _silu(x):
    h = 0.5 * x
    return h + h * jnp.tanh(h)


def _ssd_proj_kernel(h_ref, hp_ref, hn_ref, gain_ref, w_ref, cw_ref, cb_ref,
                     xs_ref, bm_ref, cm_ref, z_ref, dt_ref, *, d_x, d_bc, d_z, tiles_per_seq, col_tile):
    tm = h_ref.shape[0]
    halo = hp_ref.shape[0]
    pos = pl.program_id(0) % tiles_per_seq
    gain = gain_ref[...]
    has_prev = jnp.where(pos > 0, 1.0, 0.0)
    has_next = jnp.where(pos < tiles_per_seq - 1, 1.0, 0.0)
    xn = _rms(h_ref[...], gain).astype(BF16)
    xn_ext = jnp.concatenate([(_rms(hp_ref[...], gain) * has_prev).astype(BF16), xn,
                              (_rms(hn_ref[...], gain) * has_next).astype(BF16)], axis=0)
    rows = tm + 2 * halo
    pad = (D_CONV - 1) // 2

    def conv_cols(c0, c1):
        p_all = jnp.dot(xn_ext, w_ref[:, c0:c1], preferred_element_type=F32)
        outs = []
        for l0 in range(0, c1 - c0, LANES):
            p = p_all[:, l0:l0 + LANES]
            acc = None
            for tap in range(D_CONV):
                shift = pad - tap
                shifted = p if shift == 0 else pltpu.roll(p, shift % rows, 0)
                term = shifted[halo:halo + tm] * cw_ref[tap:tap + 1, c0 + l0:c0 + l0 + LANES]
                acc = term if acc is None else acc + term
            outs.append(_silu(acc + cb_ref[:, c0 + l0:c0 + l0 + LANES]))
        return jnp.concatenate(outs, axis=1)

    z0 = d_x + 2 * d_bc
    n_x = d_x // col_tile
    z_tile = d_z // n_x
    for t in range(n_x):
        c0 = t * col_tile
        xs_ref[:, c0:c0 + col_tile] = conv_cols(c0, c0 + col_tile)
        zc = z0 + t * z_tile
        z_ref[:, t * z_tile:(t + 1) * z_tile] = jnp.dot(xn, w_ref[:, zc:zc + z_tile], preferred_element_type=F32)
    bc = conv_cols(d_x, d_x + 2 * d_bc).astype(BF16)
    bm_ref[...] = bc[:, :d_bc]
    cm_ref[...] = bc[:, d_bc:]
    dt_ref[...] = jnp.dot(xn, w_ref[:, z0 + d_z:], preferred_element_type=F32)


def _ssd_proj(h, gain, w_ssd, conv_w, conv_b, *, layer, tm, seq_len, d_x, d_bc, d_z, d_dt):
    T, D = h.shape
    halo = 2 * SUBLANES
    per_tile = tm // halo
    n_halo_blocks = T // halo
    row = lambda i: (i, 0)
    const = lambda i: (0, 0)
    return pl.pallas_call(
        functools.partial(_ssd_proj_kernel, d_x=d_x, d_bc=d_bc, d_z=d_z,
                          tiles_per_seq=seq_len // tm, col_tile=2 * MXU_WIDTH),
        out_shape=(jax.ShapeDtypeStruct((T, d_x), F32),
                   jax.ShapeDtypeStruct((T, d_bc), BF16),
                   jax.ShapeDtypeStruct((T, d_bc), BF16),
                   jax.ShapeDtypeStruct((T, d_z), F32),
                   jax.ShapeDtypeStruct((T, d_dt), F32)),
        grid=(T // tm,),
        in_specs=[
            pl.BlockSpec((tm, D), row),
            pl.BlockSpec((halo, D), lambda i: (jnp.maximum(i * per_tile - 1, 0), 0)),
            pl.BlockSpec((halo, D), lambda i: (jnp.minimum((i + 1) * per_tile, n_halo_blocks - 1), 0)),
            pl.BlockSpec((1, D), const),
            pl.BlockSpec((None, D, d_x + 2 * d_bc + d_z + d_dt), lambda i: (layer, 0, 0)),
            pl.BlockSpec((D_CONV, d_x + 2 * d_bc), const),
            pl.BlockSpec((1, d_x + 2 * d_bc), const),
        ],
        out_specs=(pl.BlockSpec((tm, d_x), row),
                   pl.BlockSpec((tm, d_bc), row),
                   pl.BlockSpec((tm, d_bc), row),
                   pl.BlockSpec((tm, d_z), row),
                   pl.BlockSpec((tm, d_dt), row)),
        compiler_params=_cparams("parallel"),
        name="ssd_proj",
    )(h, h, h, gain, w_ssd, conv_w, conv_b)


def _attn_kernel(q_ref, k_ref, v_ref, gain_ref, o_ref, acc_ref, *, n_q, n_kv):
    q_per_kv = n_q // n_kv
    tq = q_ref.shape[0]
    half = tq // 2
    units = []
    for hd in range(n_q):
        if hd in (0, n_q - 1):
            units += [(hd, slice(0, half)), (hd, slice(half, tq))]
        else:
            units.append((hd, slice(0, tq)))
    for hd, rs in units:
        kv = hd // q_per_kv
        qh = q_ref[rs, hd * HEAD_DIM:(hd + 1) * HEAD_DIM]
        kh = k_ref[:, kv * HEAD_DIM:(kv + 1) * HEAD_DIM]
        vh = v_ref[:, kv * HEAD_DIM:(kv + 1) * HEAD_DIM]
        s = lax.dot_general(qh, kh, (((1,), (1,)), ((), ())), preferred_element_type=F32)
        m = jnp.max(s, axis=-1, keepdims=True)
        p = jnp.exp2(s - m)
        l = jnp.sum(p, axis=-1, keepdims=True)
        o = jnp.dot(p.astype(BF16), vh, preferred_element_type=F32)
        acc_ref[rs, hd * HEAD_DIM:(hd + 1) * HEAD_DIM] = o * (1.0 / l)
    o_ref[...] = _rms(acc_ref[...], gain_ref[...]).astype(BF16)


def _attention(q, k, v, gain, *, batch, seq_len, tq, n_q, n_kv):
    T, dq = q.shape
    dkv = k.shape[1]
    nq_tiles = seq_len // tq
    return pl.pallas_call(
        functools.partial(_attn_kernel, n_q=n_q, n_kv=n_kv),
        out_shape=jax.ShapeDtypeStruct((T, dq), BF16),
        grid=(batch, nq_tiles),
        in_specs=[
            pl.BlockSpec((tq, dq), lambda b, i: (b * nq_tiles + i, 0)),
            pl.BlockSpec((seq_len, dkv), lambda b, i: (b, 0)),
            pl.BlockSpec((seq_len, dkv), lambda b, i: (b, 0)),
            pl.BlockSpec((1, dq), lambda b, i: (0, 0)),
        ],
        out_specs=pl.BlockSpec((tq, dq), lambda b, i: (b * nq_tiles + i, 0)),
        scratch_shapes=[pltpu.VMEM((tq, dq), F32)],
        compiler_params=_cparams("parallel", "arbitrary"),
        name="attention",
    )(q, k, v, gain)


def _split3(a):
    hi = a.astype(BF16)
    r1 = a - hi.astype(F32)
    mid = r1.astype(BF16)
    lo = (r1 - mid.astype(F32)).astype(BF16)
    return hi, mid, lo


def _softplus(x):
    return jnp.maximum(x, 0.0) + jnp.log1p(jnp.exp(-jnp.abs(x)))


def _ssd_kernel(x_ref, b_ref, c_ref, z_ref, dt_ref, dtb_ref, alog_ref, dskip_ref, gain_ref,
                o_ref,
                bmt_s, y_s, xd_s, ecs_s, dec_s, st_s, *, n_chunks, heads):
    width = heads * SSD_HEAD_DIM
    pairs = width // LANES
    dec_rows = 2 * SUBLANES

    row = lax.broadcasted_iota(jnp.int32, (CHUNK, CHUNK), 0)
    col = lax.broadcasted_iota(jnp.int32, (CHUNK, CHUNK), 1)
    keep = (col <= row, col >= row)
    r2 = lax.broadcasted_iota(jnp.int32, (2 * CHUNK, CHUNK), 0)
    c2 = lax.broadcasted_iota(jnp.int32, (2 * CHUNK, CHUNK), 1)
    one_if = lambda cond: jnp.where(cond, 1.0, 0.0)
    tri2 = jnp.where(r2 < CHUNK, one_if(c2 <= r2), one_if(c2 >= r2 - CHUNK)).astype(BF16)
    r2t = lax.broadcasted_iota(jnp.int32, (CHUNK, 2 * CHUNK), 0)
    c2t = lax.broadcasted_iota(jnp.int32, (CHUNK, 2 * CHUNK), 1)
    tri2t = jnp.where(c2t < CHUNK, one_if(r2t <= c2t), one_if(r2t >= c2t - CHUNK)).astype(BF16)
    sel_r = lax.broadcasted_iota(jnp.int32, (LANES, 2 * width), 0)
    sel_c = lax.broadcasted_iota(jnp.int32, (LANES, 2 * width), 1)
    head_lane = (sel_c // width) * heads + (sel_c % width) // SSD_HEAD_DIM
    sel2 = jnp.where(sel_r < 4 * heads, one_if(sel_r % (2 * heads) == head_lane), 0.0).astype(BF16)
    lane_c = lax.broadcasted_iota(jnp.int32, (CHUNK, LANES), 1)
    lane_x = lane_c
    neg_a2 = -jnp.exp(alog_ref[...]) * LOG2E

    def expand(src):
        lane = lax.broadcasted_iota(jnp.int32, src.shape, 1)
        hi = src.astype(BF16).astype(F32)
        packed = jnp.where(lane < 2 * heads, hi, pltpu.roll(src - hi, 2 * heads, 1))
        return jnp.dot(packed.astype(BF16), sel2, preferred_element_type=F32)

    def stage_a(c, carry):
        r0 = pl.multiple_of(c * CHUNK, CHUNK)
        xs = x_ref[pl.ds(r0, CHUNK), :]
        cm = c_ref[pl.ds(r0, CHUNK), :]
        bm_t = b_ref[pl.ds(r0, CHUNK), :].astype(F32).T.astype(BF16)
        bmt_s[c] = bm_t

        dt = _softplus(dt_ref[pl.ds(r0, CHUNK), :] + dtb_ref[...])
        a2 = dt * neg_a2
        cs3 = jnp.dot(tri2, jnp.concatenate(_split3(a2), axis=1), preferred_element_type=F32)
        cs2 = cs3[:, :LANES] + cs3[:, LANES:2 * LANES] + cs3[:, 2 * LANES:]
        cst3 = jnp.dot(jnp.concatenate(_split3(a2.T), axis=0), tri2t, preferred_element_type=F32)
        cst2 = cst3[:CHUNK] + cst3[CHUNK:2 * CHUNK] + cst3[2 * CHUNK:]
        dt_t = dt.T

        is_fwd = lane_c < heads
        cs = jnp.where(is_fwd, cs2[:CHUNK], cs2[CHUNK:])
        total = jnp.where(is_fwd[:1], cs2[CHUNK - 1:CHUNK], cs2[CHUNK:CHUNK + 1])
        w = dt * jnp.exp2(total - cs)
        dec = jnp.broadcast_to(jnp.exp2(total), (dec_rows, LANES))
        ex = expand(jnp.concatenate([w, jnp.exp2(cs), dec], axis=0))
        for d in range(2):
            sl = slice(d * width, (d + 1) * width)
            xd_s[d, pl.ds(r0, CHUNK), :] = (xs * ex[:CHUNK, sl]).astype(BF16)
            ecs_s[d, pl.ds(r0, CHUNK), :] = ex[CHUNK:2 * CHUNK, sl]
            dec_s[d, pl.ds(pl.multiple_of(c * SUBLANES, SUBLANES), SUBLANES), :] = (
                ex[2 * CHUNK:2 * CHUNK + SUBLANES, sl])

        x_bf = xs.astype(BF16)
        cb = jnp.dot(cm, bm_t, preferred_element_type=F32)
        pieces = []
        for p in range(pairs):
            xp = x_bf[:, p * LANES:(p + 1) * LANES]
            zero = jnp.zeros_like(xp)
            x_bd = jnp.concatenate([jnp.where(lane_x < SSD_HEAD_DIM, xp, zero),
                                    jnp.where(lane_x >= SSD_HEAD_DIM, xp, zero)], axis=0)
            ms = []
            for d in range(2):
                for hh in (2 * p, 2 * p + 1):
                    j = d * heads + hh
                    diff = (cs2[d * CHUNK:(d + 1) * CHUNK, j:j + 1]
                            - cst2[j:j + 1, d * CHUNK:(d + 1) * CHUNK])
                    lm = jnp.exp2(jnp.where(keep[d], diff, -jnp.inf))
                    ms.append((cb * lm * dt_t[j:j + 1, :]).astype(BF16))
            pieces.append(jnp.dot(jnp.concatenate(ms, axis=1),
                                  jnp.concatenate([x_bd, x_bd], axis=0),
                                  preferred_element_type=F32))
        y_s[pl.ds(r0, CHUNK), :] = jnp.concatenate(pieces, axis=1) + dskip_ref[...] * xs
        return carry

    lax.fori_loop(0, n_chunks, stage_a, 0, unroll=8)

    st_s[...] = jnp.zeros_like(st_s)

    def stage_b(i, carry):
        for d, c in ((0, i), (1, n_chunks - 1 - i)):
            r0 = pl.multiple_of(c * CHUNK, CHUNK)
            state = st_s[d]
            y_off = jnp.dot(c_ref[pl.ds(r0, CHUNK), :], state.astype(BF16),
                            preferred_element_type=F32) * ecs_s[d, pl.ds(r0, CHUNK), :]
            y_s[pl.ds(r0, CHUNK), :] += y_off
            new = jnp.dot(bmt_s[c], xd_s[d, pl.ds(r0, CHUNK), :], preferred_element_type=F32)
            decay = dec_s[d, pl.ds(pl.multiple_of(c * SUBLANES, SUBLANES), SUBLANES), :][:1]
            st_s[d] = state * decay + new
        return carry

    lax.fori_loop(0, n_chunks, stage_b, 0, unroll=8)

    def stage_c(c, carry):
        r0 = pl.multiple_of(c * CHUNK, CHUNK)
        y = y_s[pl.ds(r0, CHUNK), :] * _silu(z_ref[pl.ds(r0, CHUNK), :])
        o_ref[pl.ds(r0, CHUNK), :] = _rms(y, gain_ref[...]).astype(BF16)
        return carry

    lax.fori_loop(0, n_chunks, stage_c, 0, unroll=4)


def _ssd(xs, bm, cm, z, dt, dt_bias, a_log, d_skip, gain, *, batch, seq_len):
    T, d_ssd = z.shape
    width = d_ssd // N_GROUPS
    heads = width // SSD_HEAD_DIM
    n_chunks = seq_len // CHUNK

    def seq(cols):
        return pl.BlockSpec((seq_len, cols), lambda b, g: (b, g))

    def par(cols):
        return pl.BlockSpec((1, cols), lambda b, g: (0, g))

    return pl.pallas_call(
        functools.partial(_ssd_kernel, n_chunks=n_chunks, heads=heads),
        out_shape=jax.ShapeDtypeStruct((T, d_ssd), BF16),
        grid=(batch, N_GROUPS),
        in_specs=[
            seq(width), seq(D_STATE), seq(D_STATE), seq(width), seq(LANES),
            par(LANES), par(LANES), par(width), par(width),
        ],
        out_specs=seq(width),
        scratch_shapes=[
            pltpu.VMEM((n_chunks, D_STATE, CHUNK), BF16),
            pltpu.VMEM((seq_len, width), F32),
            pltpu.VMEM((2, seq_len, width), BF16),
            pltpu.VMEM((2, seq_len, width), F32),
            pltpu.VMEM((2, n_chunks * SUBLANES, width), F32),
            pltpu.VMEM((2, D_STATE, width), F32),
        ],
        compiler_params=_cparams("parallel", "arbitrary"),
        name="ssd",
    )(xs, bm, cm, z, dt, dt_bias, a_log, d_skip, gain)


def _out_kernel(a_ref, s_ref, h_ref, wa_ref, ws_ref, o_ref):
    o_ref[...] = (h_ref[...]
                  + jnp.dot(a_ref[...], wa_ref[...], preferred_element_type=F32)
                  + jnp.dot(s_ref[...], ws_ref[...], preferred_element_type=F32))


def _out_proj(a, s, h, w_out, *, layer, tm):
    T, D = h.shape
    da, ds = a.shape[1], s.shape[1]
    row = lambda i: (i, 0)
    return pl.pallas_call(
        _out_kernel,
        out_shape=jax.ShapeDtypeStruct((T, D), F32),
        grid=(T // tm,),
        in_specs=[
            pl.BlockSpec((tm, da), row),
            pl.BlockSpec((tm, ds), row),
            pl.BlockSpec((tm, D), row),
            pl.BlockSpec((None, da, D), lambda i: (layer, 0, 0)),
            pl.BlockSpec((None, ds, D), lambda i: (layer, da // ds, 0)),
        ],
        out_specs=pl.BlockSpec((tm, D), row),
        compiler_params=_cparams("parallel"),
        name="out_proj",
    )(a, s, h, w_out, w_out)


def _rope_tables(seq_len):
    axis_dim = HEAD_DIM // 2
    t = jnp.arange(seq_len)
    row_pos = (t // GRID_W).astype(F32)
    col_pos = (t % GRID_W).astype(F32)
    inv_freq = ROPE_THETA ** (-jnp.arange(0, axis_dim, 2, dtype=F32) / axis_dim)
    ang = jnp.concatenate([row_pos[:, None] * inv_freq, col_pos[:, None] * inv_freq], axis=-1)
    cos = jnp.concatenate([jnp.cos(ang), jnp.cos(ang)], axis=-1)
    sin = jnp.concatenate([-jnp.sin(ang), jnp.sin(ang)], axis=-1)
    return cos, sin


def _rope_layout(t):
    quarter = HEAD_DIM // 4
    lead = t.shape[:-1]
    t = t.reshape(lead + (-1, 2, 2, quarter))
    return jnp.swapaxes(t, -3, -2).reshape(lead + (-1,))


def _group_dt_columns(t, n_heads):
    hpg = n_heads // N_GROUPS
    lead = t.shape[:-1]
    t = t.reshape(lead + (2, N_GROUPS, hpg))
    t = jnp.moveaxis(t, -3, -2).reshape(lead + (N_GROUPS, 2 * hpg))
    t = jnp.pad(t, [(0, 0)] * (len(lead) + 1) + [(0, LANES - 2 * hpg)])
    return t.reshape(lead + (N_GROUPS * LANES,))


def kernel(x, ffn1_norm, ffn1_w_gu, ffn1_w_down, mix_norm, w_in, conv_w, conv_b, dt_bias, a_log, d_skip, q_norm, k_norm, attn_out_norm, ssd_out_norm, w_out, ffn2_norm, ffn2_w_gu, ffn2_w_down, final_norm):
    batch, seq_len, d_model = x.shape
    depth = ffn1_norm.shape[0]
    d_attn = attn_out_norm.shape[1]
    d_ssd = ssd_out_norm.shape[1]
    n_ssd_heads = d_skip.shape[1]
    n_q = d_attn // HEAD_DIM
    d_kv = N_KV_HEADS * HEAD_DIM
    d_xbc = conv_w.shape[2]
    d_qkv = d_attn + 2 * d_kv
    d_dt = N_GROUPS * LANES

    T = batch * seq_len
    tm = _token_tile(seq_len)
    tm_ffn, tf = _ffn_tiles(T, ffn1_w_down.shape[1])
    h = x.reshape(T, d_model)
    cos, sin = _rope_tables(seq_len)
    row = lambda v: v.reshape(1, -1)
    final_gain = row(final_norm)

    w_gu1, w_dn1 = ffn1_w_gu.astype(BF16), ffn1_w_down.astype(BF16)
    w_gu2, w_dn2 = ffn2_w_gu.astype(BF16), ffn2_w_down.astype(BF16)
    d_qk = d_attn + d_kv
    w_qkv = jnp.concatenate([_rope_layout(w_in[:, :, :d_qk]), w_in[:, :, d_qk:d_qkv]], axis=-1).astype(BF16)
    q_gain, k_gain = _rope_layout(q_norm), _rope_layout(k_norm)
    z0 = d_qkv
    xbc0 = z0 + d_ssd
    dt0 = xbc0 + d_xbc
    w_ssd = jnp.concatenate([w_in[:, :, xbc0:dt0], w_in[:, :, z0:xbc0],
                             _group_dt_columns(w_in[:, :, dt0:], n_ssd_heads)], axis=-1).astype(BF16)
    w_o = w_out.astype(BF16)
    dt_bias_g = _group_dt_columns(dt_bias.reshape(depth, -1), n_ssd_heads)
    a_log_g = _group_dt_columns(a_log.reshape(depth, -1), n_ssd_heads)
    d_skip_e = jnp.repeat(d_skip, SSD_HEAD_DIM, axis=-1)

    for i in range(depth):
        last = i == depth - 1
        h = _ffn(h, row(ffn1_norm[i]), w_gu1, w_dn1, final_gain,
                 layer=i, tm=tm_ffn, tf=tf, final_norm=False)

        gain = row(mix_norm[i])
        q, k, v = _qkv_proj(h, gain, w_qkv, cos, sin, row(q_gain[i]), row(k_gain[i]),
                            layer=i, tm=tm, seq_len=seq_len, n_q=n_q, n_kv=N_KV_HEADS)
        xs, bm, cm, z, dt = _ssd_proj(h, gain, w_ssd, conv_w[i], row(conv_b[i]), layer=i, tm=tm,
                                      seq_len=seq_len, d_x=d_ssd, d_bc=(d_xbc - d_ssd) // 2,
                                      d_z=d_ssd, d_dt=d_dt)
        a_out = _attention(q, k, v, row(attn_out_norm[i]), batch=batch, seq_len=seq_len,
                           tq=tm, n_q=n_q, n_kv=N_KV_HEADS)
        s_out = _ssd(xs, bm, cm, z, dt, row(dt_bias_g[i]), row(a_log_g[i]),
                     row(d_skip_e[i]), row(ssd_out_norm[i]), batch=batch, seq_len=seq_len)
        h = _out_proj(a_out, s_out, h, w_o, layer=i, tm=tm)

        h = _ffn(h, row(ffn2_norm[i]), w_gu2, w_dn2, final_gain,
                 layer=i, tm=tm_ffn, tf=tf, final_norm=last)
    return h.reshape(batch, seq_len, d_model)
```

```python
import functools
import math

import jax
import jax.numpy as jnp
from jax import lax
from jax.experimental import pallas as pl
from jax.experimental.pallas import tpu as pltpu

F32 = jnp.float32
BF16 = jnp.bfloat16

EPS = 1e-6
GRID_W = 64
ROPE_THETA = 10000.0
HEAD_DIM = 128
N_KV_HEADS = 2
SSD_HEAD_DIM = 64
N_GROUPS = 2
D_STATE = 128
D_CONV = 5
CHUNK = 128
LANES = 128
SUBLANES = 8
MXU_WIDTH = 256
CAST_ROWS = 2 * SUBLANES
LOG2E = math.log2(math.e)
VMEM_LIMIT = 56 * 1024 * 1024


def _cparams(*sem):
    return pltpu.CompilerParams(dimension_semantics=sem, vmem_limit_bytes=VMEM_LIMIT)


def _token_tile(seq_len):
    return min(512, seq_len)


def _ffn_tiles(n_tokens, d_ff):
    tm = 1024 if n_tokens % 1024 == 0 else 512
    tf = 512 if d_ff % 512 == 0 else LANES
    return tm, tf


def _rms(x, gain):
    r = lax.rsqrt(jnp.mean(x * x, axis=-1, keepdims=True) + EPS)
    return x * r * gain


def _ffn_kernel(*refs, tm, prefetch_step, final_norm, n_cast):
    x_hbm, gain_ref, wg_ref, wu_ref, wd_ref, fgain_ref = refs[:6]
    cast_in = refs[6:6 + n_cast]
    o_ref = refs[6 + n_cast]
    cast_out = refs[7 + n_cast:7 + 2 * n_cast]
    x_s, xn_ref, sem = refs[7 + 2 * n_cast:]
    i = pl.program_id(0)
    j = pl.program_id(1)

    for src_ref, dst_ref in zip(cast_in, cast_out):
        dst_ref[...] = src_ref[...].astype(BF16)

    def x_copy(tile):
        return pltpu.make_async_copy(x_hbm.at[pl.ds(tile * tm, tm), :], x_s, sem)

    @pl.when(j == 0)
    def _():
        @pl.when(i == 0)
        def _():
            x_copy(0).start()

        x_copy(i).wait()
        x = x_s[...]
        xn_ref[...] = _rms(x, gain_ref[...]).astype(BF16)
        o_ref[...] = x

    @pl.when((j == prefetch_step) & (i + 1 < pl.num_programs(0)))
    def _():
        x_copy(i + 1).start()

    xn = xn_ref[...]
    g = jnp.dot(xn, wg_ref[...], preferred_element_type=F32)
    u = jnp.dot(xn, wu_ref[...], preferred_element_type=F32)
    h = (g * jax.nn.sigmoid(g) * (0.5 * u)).astype(BF16)
    o_ref[...] += jnp.dot(h, wd_ref[...], preferred_element_type=F32)

    if final_norm:
        @pl.when(j == pl.num_programs(1) - 1)
        def _():
            o_ref[...] = _rms(o_ref[...], fgain_ref[...])


def _ffn(h, gain, w_gu, w_down, final_gain, next_weights, *, tm, tf, final_norm):
    T, D = h.shape
    FF = w_down.shape[0]
    nj = FF // tf
    n_steps = (T // tm) * nj
    slab_rows = n_steps * CAST_ROWS
    cast_ops, cast_specs, cast_shapes, cast_out_specs, done = [], [], [], [], []
    for stack, layer in next_weights:
        n = stack.shape[1] * stack.shape[2]
        cols = n // slab_rows
        if n % slab_rows or cols % LANES:
            done.append(stack[layer].astype(BF16))
            continue
        done.append(None)
        cast_ops.append(stack.reshape(stack.shape[0] * slab_rows, cols))
        cast_specs.append(pl.BlockSpec((CAST_ROWS, cols),
                                       lambda i, j, layer=layer: (layer * n_steps + i * nj + j, 0)))
        cast_shapes.append(jax.ShapeDtypeStruct((slab_rows, cols), BF16))
        cast_out_specs.append(pl.BlockSpec((CAST_ROWS, cols), lambda i, j: (i * nj + j, 0)))
    outs = pl.pallas_call(
        functools.partial(_ffn_kernel, tm=tm, prefetch_step=min(2, nj - 1), final_norm=final_norm,
                          n_cast=len(cast_ops)),
        out_shape=[jax.ShapeDtypeStruct((T, D), F32)] + cast_shapes,
        grid=(T // tm, nj),
        in_specs=[
            pl.BlockSpec(memory_space=pl.ANY),
            pl.BlockSpec((1, D), lambda i, j: (0, 0)),
            pl.BlockSpec((D, tf), lambda i, j: (0, j)),
            pl.BlockSpec((D, tf), lambda i, j: (0, j + nj)),
            pl.BlockSpec((tf, D), lambda i, j: (j, 0)),
            pl.BlockSpec((1, D), lambda i, j: (0, 0)),
        ] + cast_specs,
        out_specs=[pl.BlockSpec((tm, D), lambda i, j: (i, 0))] + cast_out_specs,
        scratch_shapes=[pltpu.VMEM((tm, D), F32), pltpu.VMEM((tm, D), BF16), pltpu.SemaphoreType.DMA(())],
        compiler_params=_cparams("arbitrary", "arbitrary"),
        name="ffn_final" if final_norm else "ffn",
    )(h, gain, w_gu, w_gu, w_down, final_gain, *cast_ops)
    cast_iter = iter(outs[1:])
    bf16_next = [d if d is not None else next(cast_iter).reshape(stack.shape[1:])
                 for d, (stack, _) in zip(done, next_weights)]
    return outs[0], bf16_next


def _rope(x, cos, sin_signed):
    return x * cos + pltpu.roll(x, HEAD_DIM // 2, 1) * sin_signed


def _qkv_kernel(h_ref, gain_ref, w_ref, cos_ref, sin_ref, qg_ref, kg_ref,
                q_ref, k_ref, v_ref, *, n_q, n_kv, q_scale):
    xn = _rms(h_ref[...], gain_ref[...]).astype(BF16)
    cos = cos_ref[...]
    sin = sin_ref[...]
    heads_per_dot = MXU_WIDTH // HEAD_DIM
    n_heads = n_q + 2 * n_kv
    for g in range(n_heads // heads_per_dot):
        c0 = g * MXU_WIDTH
        p = jnp.dot(xn, w_ref[:, c0:c0 + MXU_WIDTH], preferred_element_type=F32)
        for sub in range(heads_per_dot):
            hd = g * heads_per_dot + sub
            ph = p[:, sub * HEAD_DIM:(sub + 1) * HEAD_DIM]
            if hd < n_q:
                qh = _rope(_rms(ph, qg_ref[...]), cos, sin) * q_scale
                q_ref[:, hd * HEAD_DIM:(hd + 1) * HEAD_DIM] = qh.astype(BF16)
            elif hd < n_q + n_kv:
                kh = _rope(_rms(ph, kg_ref[...]), cos, sin)
                kk = hd - n_q
                k_ref[:, kk * HEAD_DIM:(kk + 1) * HEAD_DIM] = kh.astype(BF16)
            else:
                vv = hd - n_q - n_kv
                v_ref[:, vv * HEAD_DIM:(vv + 1) * HEAD_DIM] = ph.astype(BF16)


def _qkv_proj(h, gain, w_qkv, cos, sin, q_gain, k_gain, *, layer, tm, seq_len, n_q, n_kv):
    T, D = h.shape
    tiles_per_seq = seq_len // tm
    dq, dkv = n_q * HEAD_DIM, n_kv * HEAD_DIM
    row = lambda i: (i, 0)
    const = lambda i: (0, 0)
    pos = lambda i: (i % tiles_per_seq, 0)
    return pl.pallas_call(
        functools.partial(_qkv_kernel, n_q=n_q, n_kv=n_kv, q_scale=HEAD_DIM ** -0.5 * LOG2E),
        out_shape=(jax.ShapeDtypeStruct((T, dq), BF16),
                   jax.ShapeDtypeStruct((T, dkv), BF16),
                   jax.ShapeDtypeStruct((T, dkv), BF16)),
        grid=(T // tm,),
        in_specs=[
            pl.BlockSpec((tm, D), row),
            pl.BlockSpec((1, D), const),
            pl.BlockSpec((None, D, dq + 2 * dkv), lambda i: (layer, 0, 0)),
            pl.BlockSpec((tm, HEAD_DIM), pos),
            pl.BlockSpec((tm, HEAD_DIM), pos),
            pl.BlockSpec((1, HEAD_DIM), const),
            pl.BlockSpec((1, HEAD_DIM), const),
        ],
        out_specs=(pl.BlockSpec((tm, dq), row),
                   pl.BlockSpec((tm, dkv), row),
                   pl.BlockSpec((tm, dkv), row)),
        compiler_params=_cparams("parallel"),
        name="qkv_proj",
    )(h, gain, w_qkv, cos, sin, q_gain, k_gain)


def _silu(x):
    h = 0.5 * x
    return h + h * jnp.tanh(h)


def _ssd_proj_kernel(h_ref, hp_ref, hn_ref, gain_ref, w_ref, cw_ref, cb_ref,
                     xs_ref, bm_ref, cm_ref, z_ref, dt_ref, *, d_x, d_bc, d_z, tiles_per_seq, col_tile):
    tm = h_ref.shape[0]
    halo = hp_ref.shape[0]
    pos = pl.program_id(0) % tiles_per_seq
    gain = gain_ref[...]
    has_prev = jnp.where(pos > 0, 1.0, 0.0)
    has_next = jnp.where(pos < tiles_per_seq - 1, 1.0, 0.0)
    xn = _rms(h_ref[...], gain).astype(BF16)
    xn_ext = jnp.concatenate([(_rms(hp_ref[...], gain) * has_prev).astype(BF16), xn,
                              (_rms(hn_ref[...], gain) * has_next).astype(BF16)], axis=0)
    rows = tm + 2 * halo
    pad = (D_CONV - 1) // 2

    def conv_cols(c0, c1):
        p_all = jnp.dot(xn_ext, w_ref[:, c0:c1], preferred_element_type=F32)
        outs = []
        for l0 in range(0, c1 - c0, LANES):
            p = p_all[:, l0:l0 + LANES]
            acc = None
            for tap in range(D_CONV):
                shift = pad - tap
                shifted = p if shift == 0 else pltpu.roll(p, shift % rows, 0)
                term = shifted[halo:halo + tm] * cw_ref[tap:tap + 1, c0 + l0:c0 + l0 + LANES]
                acc = term if acc is None else acc + term
            outs.append(_silu(acc + cb_ref[:, c0 + l0:c0 + l0 + LANES]))
        return jnp.concatenate(outs, axis=1)

    z0 = d_x + 2 * d_bc
    n_x = d_x // col_tile
    z_tile = d_z // n_x
    for t in range(n_x):
        c0 = t * col_tile
        xs_ref[:, c0:c0 + col_tile] = conv_cols(c0, c0 + col_tile)
        zc = z0 + t * z_tile
        z_ref[:, t * z_tile:(t + 1) * z_tile] = jnp.dot(xn, w_ref[:, zc:zc + z_tile], preferred_element_type=F32)
    bc = conv_cols(d_x, d_x + 2 * d_bc).astype(BF16)
    bm_ref[...] = bc[:, :d_bc]
    cm_ref[...] = bc[:, d_bc:]
    dt_ref[...] = jnp.dot(xn, w_ref[:, z0 + d_z:], preferred_element_type=F32)


def _ssd_proj(h, gain, w_ssd, conv_w, conv_b, *, layer, tm, seq_len, d_x, d_bc, d_z, d_dt):
    T, D = h.shape
    halo = 2 * SUBLANES
    per_tile = tm // halo
    n_halo_blocks = T // halo
    row = lambda i: (i, 0)
    const = lambda i: (0, 0)
    return pl.pallas_call(
        functools.partial(_ssd_proj_kernel, d_x=d_x, d_bc=d_bc, d_z=d_z,
                          tiles_per_seq=seq_len // tm, col_tile=2 * MXU_WIDTH),
        out_shape=(jax.ShapeDtypeStruct((T, d_x), F32),
                   jax.ShapeDtypeStruct((T, d_bc), BF16),
                   jax.ShapeDtypeStruct((T, d_bc), BF16),
                   jax.ShapeDtypeStruct((T, d_z), F32),
                   jax.ShapeDtypeStruct((T, d_dt), F32)),
        grid=(T // tm,),
        in_specs=[
            pl.BlockSpec((tm, D), row),
            pl.BlockSpec((halo, D), lambda i: (jnp.maximum(i * per_tile - 1, 0), 0)),
            pl.BlockSpec((halo, D), lambda i: (jnp.minimum((i + 1) * per_tile, n_halo_blocks - 1), 0)),
            pl.BlockSpec((1, D), const),
            pl.BlockSpec((None, D, d_x + 2 * d_bc + d_z + d_dt), lambda i: (layer, 0, 0)),
            pl.BlockSpec((D_CONV, d_x + 2 * d_bc), const),
            pl.BlockSpec((1, d_x + 2 * d_bc), const),
        ],
        out_specs=(pl.BlockSpec((tm, d_x), row),
                   pl.BlockSpec((tm, d_bc), row),
                   pl.BlockSpec((tm, d_bc), row),
                   pl.BlockSpec((tm, d_z), row),
                   pl.BlockSpec((tm, d_dt), row)),
        compiler_params=_cparams("parallel"),
        name="ssd_proj",
    )(h, h, h, gain, w_ssd, conv_w, conv_b)


def _attn_kernel(q_ref, k_ref, v_ref, gain_ref, o_ref, acc_ref, *, n_q, n_kv):
    q_per_kv = n_q // n_kv
    tq = q_ref.shape[0]
    half = tq // 2
    units = []
    for hd in range(n_q):
        if hd in (0, n_q - 1):
            units += [(hd, slice(0, half)), (hd, slice(half, tq))]
        else:
            units.append((hd, slice(0, tq)))
    for hd, rs in units:
        kv = hd // q_per_kv
        qh = q_ref[rs, hd * HEAD_DIM:(hd + 1) * HEAD_DIM]
        kh = k_ref[:, kv * HEAD_DIM:(kv + 1) * HEAD_DIM]
        vh = v_ref[:, kv * HEAD_DIM:(kv + 1) * HEAD_DIM]
        s = lax.dot_general(qh, kh, (((1,), (1,)), ((), ())), preferred_element_type=F32)
        m = jnp.max(s, axis=-1, keepdims=True)
        p = jnp.exp2(s - m)
        l = jnp.sum(p, axis=-1, keepdims=True)
        o = jnp.dot(p.astype(BF16), vh, preferred_element_type=F32)
        acc_ref[rs, hd * HEAD_DIM:(hd + 1) * HEAD_DIM] = o * (1.0 / l)
    o_ref[...] = _rms(acc_ref[...], gain_ref[...]).astype(BF16)


def _attention(q, k, v, gain, *, batch, seq_len, tq, n_q, n_kv):
    T, dq = q.shape
    dkv = k.shape[1]
    nq_tiles = seq_len // tq
    return pl.pallas_call(
        functools.partial(_attn_kernel, n_q=n_q, n_kv=n_kv),
        out_shape=jax.ShapeDtypeStruct((T, dq), BF16),
        grid=(batch, nq_tiles),
        in_specs=[
            pl.BlockSpec((tq, dq), lambda b, i: (b * nq_tiles + i, 0)),
            pl.BlockSpec((seq_len, dkv), lambda b, i: (b, 0)),
            pl.BlockSpec((seq_len, dkv), lambda b, i: (b, 0)),
            pl.BlockSpec((1, dq), lambda b, i: (0, 0)),
        ],
        out_specs=pl.BlockSpec((tq, dq), lambda b, i: (b * nq_tiles + i, 0)),
        scratch_shapes=[pltpu.VMEM((tq, dq), F32)],
        compiler_params=_cparams("parallel", "arbitrary"),
        name="attention",
    )(q, k, v, gain)


def _split3(a):
    hi = a.astype(BF16)
    r1 = a - hi.astype(F32)
    mid = r1.astype(BF16)
    lo = (r1 - mid.astype(F32)).astype(BF16)
    return hi, mid, lo


def _softplus(x):
    return jnp.maximum(x, 0.0) + jnp.log1p(jnp.exp(-jnp.abs(x)))


def _ssd_kernel(x_ref, b_ref, c_ref, z_ref, dt_ref, dtb_ref, alog_ref, dskip_ref, gain_ref,
                o_ref,
                bmt_s, y_s, xd_s, ecs_s, dec_s, st_s, *, n_chunks, heads):
    width = heads * SSD_HEAD_DIM
    pairs = width // LANES
    dec_rows = 2 * SUBLANES

    row = lax.broadcasted_iota(jnp.int32, (CHUNK, CHUNK), 0)
    col = lax.broadcasted_iota(jnp.int32, (CHUNK, CHUNK), 1)
    keep = (col <= row, col >= row)
    r2 = lax.broadcasted_iota(jnp.int32, (2 * CHUNK, CHUNK), 0)
    c2 = lax.broadcasted_iota(jnp.int32, (2 * CHUNK, CHUNK), 1)
    one_if = lambda cond: jnp.where(cond, 1.0, 0.0)
    tri2 = jnp.where(r2 < CHUNK, one_if(c2 <= r2), one_if(c2 >= r2 - CHUNK)).astype(BF16)
    r2t = lax.broadcasted_iota(jnp.int32, (CHUNK, 2 * CHUNK), 0)
    c2t = lax.broadcasted_iota(jnp.int32, (CHUNK, 2 * CHUNK), 1)
    tri2t = jnp.where(c2t < CHUNK, one_if(r2t <= c2t), one_if(r2t >= c2t - CHUNK)).astype(BF16)
    sel_r = lax.broadcasted_iota(jnp.int32, (LANES, 2 * width), 0)
    sel_c = lax.broadcasted_iota(jnp.int32, (LANES, 2 * width), 1)
    head_lane = (sel_c // width) * heads + (sel_c % width) // SSD_HEAD_DIM
    sel2 = jnp.where(sel_r < 4 * heads, one_if(sel_r % (2 * heads) == head_lane), 0.0).astype(BF16)
    lane_c = lax.broadcasted_iota(jnp.int32, (CHUNK, LANES), 1)
    lane_x = lane_c
    neg_a2 = -jnp.exp(alog_ref[...]) * LOG2E

    def expand(src):
        lane = lax.broadcasted_iota(jnp.int32, src.shape, 1)
        hi = src.astype(BF16).astype(F32)
        packed = jnp.where(lane < 2 * heads, hi, pltpu.roll(src - hi, 2 * heads, 1))
        return jnp.dot(packed.astype(BF16), sel2, preferred_element_type=F32)

    def stage_a(c, carry):
        r0 = pl.multiple_of(c * CHUNK, CHUNK)
        xs = x_ref[pl.ds(r0, CHUNK), :]
        cm = c_ref[pl.ds(r0, CHUNK), :]
        bm_t = b_ref[pl.ds(r0, CHUNK), :].astype(F32).T.astype(BF16)
        bmt_s[c] = bm_t

        dt = _softplus(dt_ref[pl.ds(r0, CHUNK), :] + dtb_ref[...])
        a2 = dt * neg_a2
        cs3 = jnp.dot(tri2, jnp.concatenate(_split3(a2), axis=1), preferred_element_type=F32)
        cs2 = cs3[:, :LANES] + cs3[:, LANES:2 * LANES] + cs3[:, 2 * LANES:]
        cst3 = jnp.dot(jnp.concatenate(_split3(a2.T), axis=0), tri2t, preferred_element_type=F32)
        cst2 = cst3[:CHUNK] + cst3[CHUNK:2 * CHUNK] + cst3[2 * CHUNK:]
        dt_t = dt.T

        is_fwd = lane_c < heads
        cs = jnp.where(is_fwd, cs2[:CHUNK], cs2[CHUNK:])
        total = jnp.where(is_fwd[:1], cs2[CHUNK - 1:CHUNK], cs2[CHUNK:CHUNK + 1])
        w = dt * jnp.exp2(total - cs)
        dec = jnp.broadcast_to(jnp.exp2(total), (dec_rows, LANES))
        ex = expand(jnp.concatenate([w, jnp.exp2(cs), dec], axis=0))
        for d in range(2):
            sl = slice(d * width, (d + 1) * width)
            xd_s[d, pl.ds(r0, CHUNK), :] = (xs * ex[:CHUNK, sl]).astype(BF16)
            ecs_s[d, pl.ds(r0, CHUNK), :] = ex[CHUNK:2 * CHUNK, sl]
            dec_s[d, pl.ds(pl.multiple_of(c * SUBLANES, SUBLANES), SUBLANES), :] = (
                ex[2 * CHUNK:2 * CHUNK + SUBLANES, sl])

        x_bf = xs.astype(BF16)
        cb = jnp.dot(cm, bm_t, preferred_element_type=F32)
        pieces = []
        for p in range(pairs):
            xp = x_bf[:, p * LANES:(p + 1) * LANES]
            zero = jnp.zeros_like(xp)
            x_bd = jnp.concatenate([jnp.where(lane_x < SSD_HEAD_DIM, xp, zero),
                                    jnp.where(lane_x >= SSD_HEAD_DIM, xp, zero)], axis=0)
            ms = []
            for d in range(2):
                for hh in (2 * p, 2 * p + 1):
                    j = d * heads + hh
                    diff = (cs2[d * CHUNK:(d + 1) * CHUNK, j:j + 1]
                            - cst2[j:j + 1, d * CHUNK:(d + 1) * CHUNK])
                    lm = jnp.exp2(jnp.where(keep[d], diff, -jnp.inf))
                    ms.append((cb * lm * dt_t[j:j + 1, :]).astype(BF16))
            pieces.append(jnp.dot(jnp.concatenate(ms, axis=1),
                                  jnp.concatenate([x_bd, x_bd], axis=0),
                                  preferred_element_type=F32))
        y_s[pl.ds(r0, CHUNK), :] = jnp.concatenate(pieces, axis=1) + dskip_ref[...] * xs
        return carry

    lax.fori_loop(0, n_chunks, stage_a, 0, unroll=8)

    st_s[...] = jnp.zeros_like(st_s)

    def stage_b(i, carry):
        for d, c in ((0, i), (1, n_chunks - 1 - i)):
            r0 = pl.multiple_of(c * CHUNK, CHUNK)
            state = st_s[d]
            y_off = jnp.dot(c_ref[pl.ds(r0, CHUNK), :], state.astype(BF16),
                            preferred_element_type=F32) * ecs_s[d, pl.ds(r0, CHUNK), :]
            y_s[pl.ds(r0, CHUNK), :] += y_off
            new = jnp.dot(bmt_s[c], xd_s[d, pl.ds(r0, CHUNK), :], preferred_element_type=F32)
            decay = dec_s[d, pl.ds(pl.multiple_of(c * SUBLANES, SUBLANES), SUBLANES), :][:1]
            st_s[d] = state * decay + new
        return carry

    lax.fori_loop(0, n_chunks, stage_b, 0, unroll=8)

    def stage_c(c, carry):
        r0 = pl.multiple_of(c * CHUNK, CHUNK)
        y = y_s[pl.ds(r0, CHUNK), :] * _silu(z_ref[pl.ds(r0, CHUNK), :])
        o_ref[pl.ds(r0, CHUNK), :] = _rms(y, gain_ref[...]).astype(BF16)
        return carry

    lax.fori_loop(0, n_chunks, stage_c, 0, unroll=4)


def _ssd(xs, bm, cm, z, dt, dt_bias, a_log, d_skip, gain, *, batch, seq_len):
    T, d_ssd = z.shape
    width = d_ssd // N_GROUPS
    heads = width // SSD_HEAD_DIM
    n_chunks = seq_len // CHUNK

    def seq(cols):
        return pl.BlockSpec((seq_len, cols), lambda b, g: (b, g))

    def par(cols):
        return pl.BlockSpec((1, cols), lambda b, g: (0, g))

    return pl.pallas_call(
        functools.partial(_ssd_kernel, n_chunks=n_chunks, heads=heads),
        out_shape=jax.ShapeDtypeStruct((T, d_ssd), BF16),
        grid=(batch, N_GROUPS),
        in_specs=[
            seq(width), seq(D_STATE), seq(D_STATE), seq(width), seq(LANES),
            par(LANES), par(LANES), par(width), par(width),
        ],
        out_specs=seq(width),
        scratch_shapes=[
            pltpu.VMEM((n_chunks, D_STATE, CHUNK), BF16),
            pltpu.VMEM((seq_len, width), F32),
            pltpu.VMEM((2, seq_len, width), BF16),
            pltpu.VMEM((2, seq_len, width), F32),
            pltpu.VMEM((2, n_chunks * SUBLANES, width), F32),
            pltpu.VMEM((2, D_STATE, width), F32),
        ],
        compiler_params=_cparams("parallel", "arbitrary"),
        name="ssd",
    )(xs, bm, cm, z, dt, dt_bias, a_log, d_skip, gain)


def _out_kernel(a_ref, s_ref, h_ref, wa_ref, ws_ref, o_ref):
    o_ref[...] = (h_ref[...]
                  + jnp.dot(a_ref[...], wa_ref[...], preferred_element_type=F32)
                  + jnp.dot(s_ref[...], ws_ref[...], preferred_element_type=F32))


def _out_proj(a, s, h, w_out, *, layer, tm):
    T, D = h.shape
    da, ds = a.shape[1], s.shape[1]
    row = lambda i: (i, 0)
    return pl.pallas_call(
        _out_kernel,
        out_shape=jax.ShapeDtypeStruct((T, D), F32),
        grid=(T // tm,),
        in_specs=[
            pl.BlockSpec((tm, da), row),
            pl.BlockSpec((tm, ds), row),
            pl.BlockSpec((tm, D), row),
            pl.BlockSpec((None, da, D), lambda i: (layer, 0, 0)),
            pl.BlockSpec((None, ds, D), lambda i: (layer, da // ds, 0)),
        ],
        out_specs=pl.BlockSpec((tm, D), row),
        compiler_params=_cparams("parallel"),
        name="out_proj",
    )(a, s, h, w_out, w_out)


def _rope_tables(seq_len):
    axis_dim = HEAD_DIM // 2
    t = jnp.arange(seq_len)
    row_pos = (t // GRID_W).astype(F32)
    col_pos = (t % GRID_W).astype(F32)
    inv_freq = ROPE_THETA ** (-jnp.arange(0, axis_dim, 2, dtype=F32) / axis_dim)
    ang = jnp.concatenate([row_pos[:, None] * inv_freq, col_pos[:, None] * inv_freq], axis=-1)
    cos = jnp.concatenate([jnp.cos(ang), jnp.cos(ang)], axis=-1)
    sin = jnp.concatenate([-jnp.sin(ang), jnp.sin(ang)], axis=-1)
    return cos, sin


def _rope_layout(t):
    quarter = HEAD_DIM // 4
    lead = t.shape[:-1]
    t = t.reshape(lead + (-1, 2, 2, quarter))
    return jnp.swapaxes(t, -3, -2).reshape(lead + (-1,))


def _group_dt_columns(t, n_heads):
    hpg = n_heads // N_GROUPS
    lead = t.shape[:-1]
    t = t.reshape(lead + (2, N_GROUPS, hpg))
    t = jnp.moveaxis(t, -3, -2).reshape(lead + (N_GROUPS, 2 * hpg))
    t = jnp.pad(t, [(0, 0)] * (len(lead) + 1) + [(0, LANES - 2 * hpg)])
    return t.reshape(lead + (N_GROUPS * LANES,))


def kernel(x, ffn1_norm, ffn1_w_gu, ffn1_w_down, mix_norm, w_in, conv_w, conv_b, dt_bias, a_log, d_skip, q_norm, k_norm, attn_out_norm, ssd_out_norm, w_out, ffn2_norm, ffn2_w_gu, ffn2_w_down, final_norm):
    batch, seq_len, d_model = x.shape
    depth = ffn1_norm.shape[0]
    d_attn = attn_out_norm.shape[1]
    d_ssd = ssd_out_norm.shape[1]
    n_ssd_heads = d_skip.shape[1]
    n_q = d_attn // HEAD_DIM
    d_kv = N_KV_HEADS * HEAD_DIM
    d_xbc = conv_w.shape[2]
    d_qkv = d_attn + 2 * d_kv
    d_dt = N_GROUPS * LANES

    T = batch * seq_len
    tm = _token_tile(seq_len)
    tm_ffn, tf = _ffn_tiles(T, ffn1_w_down.shape[1])
    h = x.reshape(T, d_model)
    cos, sin = _rope_tables(seq_len)
    row = lambda v: v.reshape(1, -1)
    final_gain = row(final_norm)

    ffn_stacks = ((ffn1_w_gu, ffn1_w_down), (ffn2_w_gu, ffn2_w_down))
    ffn_w = [ffn1_w_gu[0].astype(BF16), ffn1_w_down[0].astype(BF16)]

    def next_ffn_weights(call):
        layer, which = divmod(call, 2)
        return [(stack, layer) for stack in ffn_stacks[which]] if layer < depth else []

    d_qk = d_attn + d_kv
    w_qkv = jnp.concatenate([_rope_layout(w_in[:, :, :d_qk]), w_in[:, :, d_qk:d_qkv]], axis=-1).astype(BF16)
    q_gain, k_gain = _rope_layout(q_norm), _rope_layout(k_norm)
    z0 = d_qkv
    xbc0 = z0 + d_ssd
    dt0 = xbc0 + d_xbc
    w_ssd = jnp.concatenate([w_in[:, :, xbc0:dt0], w_in[:, :, z0:xbc0],
                             _group_dt_columns(w_in[:, :, dt0:], n_ssd_heads)], axis=-1).astype(BF16)
    w_o = w_out.astype(BF16)
    dt_bias_g = _group_dt_columns(dt_bias.reshape(depth, -1), n_ssd_heads)
    a_log_g = _group_dt_columns(a_log.reshape(depth, -1), n_ssd_heads)
    d_skip_e = jnp.repeat(d_skip, SSD_HEAD_DIM, axis=-1)

    for i in range(depth):
        last = i == depth - 1
        h, ffn_w = _ffn(h, row(ffn1_norm[i]), ffn_w[0], ffn_w[1], final_gain, next_ffn_weights(2 * i + 1),
                        tm=tm_ffn, tf=tf, final_norm=False)

        gain = row(mix_norm[i])
        q, k, v = _qkv_proj(h, gain, w_qkv, cos, sin, row(q_gain[i]), row(k_gain[i]),
                            layer=i, tm=tm, seq_len=seq_len, n_q=n_q, n_kv=N_KV_HEADS)
        xs, bm, cm, z, dt = _ssd_proj(h, gain, w_ssd, conv_w[i], row(conv_b[i]), layer=i, tm=tm,
                                      seq_len=seq_len, d_x=d_ssd, d_bc=(d_xbc - d_ssd) // 2,
                                      d_z=d_ssd, d_dt=d_dt)
        a_out = _attention(q, k, v, row(attn_out_norm[i]), batch=batch, seq_len=seq_len,
                           tq=tm, n_q=n_q, n_kv=N_KV_HEADS)
        s_out = _ssd(xs, bm, cm, z, dt, row(dt_bias_g[i]), row(a_log_g[i]),
                     row(d_skip_e[i]), row(ssd_out_norm[i]), batch=batch, seq_len=seq_len)
        h = _out_proj(a_out, s_out, h, w_o, layer=i, tm=tm)

        h, ffn_w = _ffn(h, row(ffn2_norm[i]), ffn_w[0], ffn_w[1], final_gain, next_ffn_weights(2 * i + 2),
                        tm=tm_ffn, tf=tf, final_norm=last)
    return h.reshape(batch, seq_len, d_model)
```

```python
import functools
import math

import jax
import jax.numpy as jnp
from jax import lax
from jax.experimental import pallas as pl
from jax.experimental.pallas import tpu as pltpu

F32 = jnp.float32
BF16 = jnp.bfloat16

EPS = 1e-6
GRID_W = 64
ROPE_THETA = 10000.0
HEAD_DIM = 128
N_KV_HEADS = 2
SSD_HEAD_DIM = 64
N_GROUPS = 2
D_STATE = 128
D_CONV = 5
CHUNK = 128
LANES = 128
SUBLANES = 8
MXU_WIDTH = 256
CAST_ROWS = 2 * SUBLANES
CAST_SLAB_BYTES = 1024 * 1024
LOG2E = math.log2(math.e)
VMEM_LIMIT = 56 * 1024 * 1024


def _cparams(*sem):
    return pltpu.CompilerParams(dimension_semantics=sem, vmem_limit_bytes=VMEM_LIMIT)


def _token_tile(seq_len):
    return min(512, seq_len)


def _ffn_tiles(n_tokens, d_ff):
    tm = 1024 if n_tokens % 1024 == 0 else 512
    tf = 512 if d_ff % 512 == 0 else LANES
    return tm, tf


def _rms(x, gain):
    r = lax.rsqrt(jnp.mean(x * x, axis=-1, keepdims=True) + EPS)
    return x * r * gain


def _ffn_kernel(*refs, tm, prefetch_step, final_norm, n_cast):
    x_hbm, gain_ref, wg_ref, wu_ref, wd_ref, fgain_ref = refs[:6]
    cast_in = refs[6:6 + n_cast]
    o_ref = refs[6 + n_cast]
    cast_out = refs[7 + n_cast:7 + 2 * n_cast]
    x_s, xn_ref, sem = refs[7 + 2 * n_cast:]
    i = pl.program_id(0)
    j = pl.program_id(1)

    for src_ref, dst_ref in zip(cast_in, cast_out):
        dst_ref[...] = src_ref[...].astype(BF16)

    def x_copy(tile):
        return pltpu.make_async_copy(x_hbm.at[pl.ds(tile * tm, tm), :], x_s, sem)

    @pl.when(j == 0)
    def _():
        @pl.when(i == 0)
        def _():
            x_copy(0).start()

        x_copy(i).wait()
        x = x_s[...]
        xn_ref[...] = _rms(x, gain_ref[...]).astype(BF16)
        o_ref[...] = x

    @pl.when((j == prefetch_step) & (i + 1 < pl.num_programs(0)))
    def _():
        x_copy(i + 1).start()

    xn = xn_ref[...]
    g = jnp.dot(xn, wg_ref[...], preferred_element_type=F32)
    u = jnp.dot(xn, wu_ref[...], preferred_element_type=F32)
    h = (g * jax.nn.sigmoid(g) * (0.5 * u)).astype(BF16)
    o_ref[...] += jnp.dot(h, wd_ref[...], preferred_element_type=F32)

    if final_norm:
        @pl.when(j == pl.num_programs(1) - 1)
        def _():
            o_ref[...] = _rms(o_ref[...], fgain_ref[...])


def _ffn(h, gain, w_gu, w_down, final_gain, next_weights, *, tm, tf, final_norm):
    T, D = h.shape
    FF = w_down.shape[0]
    nj = FF // tf
    n_i = T // tm
    cast_ops, cast_specs, cast_shapes, cast_out_specs, done = [], [], [], [], []
    for stack, layer in next_weights:
        _, rows, cols = stack.shape
        n_c = max([d for d in range(1, nj + 1) if cols % (d * LANES) == 0], default=0)
        rb = rows // n_i
        if rows % (n_i * CAST_ROWS) or not n_c or rb * (cols // max(n_c, 1)) * 4 > CAST_SLAB_BYTES:
            done.append(stack[layer].astype(BF16))
            continue
        cb = cols // n_c
        done.append(None)
        cast_ops.append(stack)
        cast_specs.append(pl.BlockSpec(
            (None, rb, cb), lambda i, j, layer=layer, n_c=n_c: (layer, i, jnp.minimum(j, n_c - 1))))
        cast_shapes.append(jax.ShapeDtypeStruct((rows, cols), BF16))
        cast_out_specs.append(pl.BlockSpec((rb, cb), lambda i, j, n_c=n_c: (i, jnp.minimum(j, n_c - 1))))
    outs = pl.pallas_call(
        functools.partial(_ffn_kernel, tm=tm, prefetch_step=min(2, nj - 1), final_norm=final_norm,
                          n_cast=len(cast_ops)),
        out_shape=[jax.ShapeDtypeStruct((T, D), F32)] + cast_shapes,
        grid=(T // tm, nj),
        in_specs=[
            pl.BlockSpec(memory_space=pl.ANY),
            pl.BlockSpec((1, D), lambda i, j: (0, 0)),
            pl.BlockSpec((D, tf), lambda i, j: (0, j)),
            pl.BlockSpec((D, tf), lambda i, j: (0, j + nj)),
            pl.BlockSpec((tf, D), lambda i, j: (j, 0)),
            pl.BlockSpec((1, D), lambda i, j: (0, 0)),
        ] + cast_specs,
        out_specs=[pl.BlockSpec((tm, D), lambda i, j: (i, 0))] + cast_out_specs,
        scratch_shapes=[pltpu.VMEM((tm, D), F32), pltpu.VMEM((tm, D), BF16), pltpu.SemaphoreType.DMA(())],
        compiler_params=_cparams("arbitrary", "arbitrary"),
        name="ffn_final" if final_norm else "ffn",
    )(h, gain, w_gu, w_gu, w_down, final_gain, *cast_ops)
    cast_iter = iter(outs[1:])
    bf16_next = [d if d is not None else next(cast_iter) for d in done]
    return outs[0], bf16_next


def _rope(x, cos, sin_signed):
    return x * cos + pltpu.roll(x, HEAD_DIM // 2, 1) * sin_signed


def _qkv_kernel(h_ref, gain_ref, w_ref, cos_ref, sin_ref, qg_ref, kg_ref,
                q_ref, k_ref, v_ref, *, n_q, n_kv, q_scale):
    xn = _rms(h_ref[...], gain_ref[...]).astype(BF16)
    cos = cos_ref[...]
    sin = sin_ref[...]
    heads_per_dot = MXU_WIDTH // HEAD_DIM
    n_heads = n_q + 2 * n_kv
    for g in range(n_heads // heads_per_dot):
        c0 = g * MXU_WIDTH
        p = jnp.dot(xn, w_ref[:, c0:c0 + MXU_WIDTH], preferred_element_type=F32)
        for sub in range(heads_per_dot):
            hd = g * heads_per_dot + sub
            ph = p[:, sub * HEAD_DIM:(sub + 1) * HEAD_DIM]
            if hd < n_q:
                qh = _rope(_rms(ph, qg_ref[...]), cos, sin) * q_scale
                q_ref[:, hd * HEAD_DIM:(hd + 1) * HEAD_DIM] = qh.astype(BF16)
            elif hd < n_q + n_kv:
                kh = _rope(_rms(ph, kg_ref[...]), cos, sin)
                kk = hd - n_q
                k_ref[:, kk * HEAD_DIM:(kk + 1) * HEAD_DIM] = kh.astype(BF16)
            else:
                vv = hd - n_q - n_kv
                v_ref[:, vv * HEAD_DIM:(vv + 1) * HEAD_DIM] = ph.astype(BF16)


def _qkv_proj(h, gain, w_qkv, cos, sin, q_gain, k_gain, *, layer, tm, seq_len, n_q, n_kv):
    T, D = h.shape
    tiles_per_seq = seq_len // tm
    dq, dkv = n_q * HEAD_DIM, n_kv * HEAD_DIM
    row = lambda i: (i, 0)
    const = lambda i: (0, 0)
    pos = lambda i: (i % tiles_per_seq, 0)
    return pl.pallas_call(
        functools.partial(_qkv_kernel, n_q=n_q, n_kv=n_kv, q_scale=HEAD_DIM ** -0.5 * LOG2E),
        out_shape=(jax.ShapeDtypeStruct((T, dq), BF16),
                   jax.ShapeDtypeStruct((T, dkv), BF16),
                   jax.ShapeDtypeStruct((T, dkv), BF16)),
        grid=(T // tm,),
        in_specs=[
            pl.BlockSpec((tm, D), row),
            pl.BlockSpec((1, D), const),
            pl.BlockSpec((None, D, dq + 2 * dkv), lambda i: (layer, 0, 0)),
            pl.BlockSpec((tm, HEAD_DIM), pos),
            pl.BlockSpec((tm, HEAD_DIM), pos),
            pl.BlockSpec((1, HEAD_DIM), const),
            pl.BlockSpec((1, HEAD_DIM), const),
        ],
        out_specs=(pl.BlockSpec((tm, dq), row),
                   pl.BlockSpec((tm, dkv), row),
                   pl.BlockSpec((tm, dkv), row)),
        compiler_params=_cparams("parallel"),
        name="qkv_proj",
    )(h, gain, w_qkv, cos, sin, q_gain, k_gain)


def _silu(x):
    h = 0.5 * x
    return h + h * jnp.tanh(h)


def _ssd_proj_kernel(h_ref, hp_ref, hn_ref, gain_ref, w_ref, cw_ref, cb_ref,
                     xs_ref, bm_ref, cm_ref, z_ref, dt_ref, *, d_x, d_bc, d_z, tiles_per_seq, col_tile):
    tm = h_ref.shape[0]
    halo = hp_ref.shape[0]
    pos = pl.program_id(0) % tiles_per_seq
    gain = gain_ref[...]
    has_prev = jnp.where(pos > 0, 1.0, 0.0)
    has_next = jnp.where(pos < tiles_per_seq - 1, 1.0, 0.0)
    xn = _rms(h_ref[...], gain).astype(BF16)
    xn_ext = jnp.concatenate([(_rms(hp_ref[...], gain) * has_prev).astype(BF16), xn,
                              (_rms(hn_ref[...], gain) * has_next).astype(BF16)], axis=0)
    rows = tm + 2 * halo
    pad = (D_CONV - 1) // 2

    def conv_cols(c0, c1):
        p_all = jnp.dot(xn_ext, w_ref[:, c0:c1], preferred_element_type=F32)
        outs = []
        for l0 in range(0, c1 - c0, LANES):
            p = p_all[:, l0:l0 + LANES]
            acc = None
            for tap in range(D_CONV):
                shift = pad - tap
                shifted = p if shift == 0 else pltpu.roll(p, shift % rows, 0)
                term = shifted[halo:halo + tm] * cw_ref[tap:tap + 1, c0 + l0:c0 + l0 + LANES]
                acc = term if acc is None else acc + term
            outs.append(_silu(acc + cb_ref[:, c0 + l0:c0 + l0 + LANES]))
        return jnp.concatenate(outs, axis=1)

    z0 = d_x + 2 * d_bc
    n_x = d_x // col_tile
    z_tile = d_z // n_x
    for t in range(n_x):
        c0 = t * col_tile
        xs_ref[:, c0:c0 + col_tile] = conv_cols(c0, c0 + col_tile)
        zc = z0 + t * z_tile
        z_ref[:, t * z_tile:(t + 1) * z_tile] = jnp.dot(xn, w_ref[:, zc:zc + z_tile], preferred_element_type=F32)
    bc = conv_cols(d_x, d_x + 2 * d_bc).astype(BF16)
    bm_ref[...] = bc[:, :d_bc]
    cm_ref[...] = bc[:, d_bc:]
    dt_ref[...] = jnp.dot(xn, w_ref[:, z0 + d_z:], preferred_element_type=F32)


def _ssd_proj(h, gain, w_ssd, conv_w, conv_b, *, layer, tm, seq_len, d_x, d_bc, d_z, d_dt):
    T, D = h.shape
    halo = 2 * SUBLANES
    per_tile = tm // halo
    n_halo_blocks = T // halo
    row = lambda i: (i, 0)
    const = lambda i: (0, 0)
    return pl.pallas_call(
        functools.partial(_ssd_proj_kernel, d_x=d_x, d_bc=d_bc, d_z=d_z,
                          tiles_per_seq=seq_len // tm, col_tile=2 * MXU_WIDTH),
        out_shape=(jax.ShapeDtypeStruct((T, d_x), F32),
                   jax.ShapeDtypeStruct((T, d_bc), BF16),
                   jax.ShapeDtypeStruct((T, d_bc), BF16),
                   jax.ShapeDtypeStruct((T, d_z), F32),
                   jax.ShapeDtypeStruct((T, d_dt), F32)),
        grid=(T // tm,),
        in_specs=[
            pl.BlockSpec((tm, D), row),
            pl.BlockSpec((halo, D), lambda i: (jnp.maximum(i * per_tile - 1, 0), 0)),
            pl.BlockSpec((halo, D), lambda i: (jnp.minimum((i + 1) * per_tile, n_halo_blocks - 1), 0)),
            pl.BlockSpec((1, D), const),
            pl.BlockSpec((None, D, d_x + 2 * d_bc + d_z + d_dt), lambda i: (layer, 0, 0)),
            pl.BlockSpec((D_CONV, d_x + 2 * d_bc), const),
            pl.BlockSpec((1, d_x + 2 * d_bc), const),
        ],
        out_specs=(pl.BlockSpec((tm, d_x), row),
                   pl.BlockSpec((tm, d_bc), row),
                   pl.BlockSpec((tm, d_bc), row),
                   pl.BlockSpec((tm, d_z), row),
                   pl.BlockSpec((tm, d_dt), row)),
        compiler_params=_cparams("parallel"),
        name="ssd_proj",
    )(h, h, h, gain, w_ssd, conv_w, conv_b)


def _attn_kernel(q_ref, k_ref, v_ref, gain_ref, o_ref, acc_ref, *, n_q, n_kv):
    q_per_kv = n_q // n_kv
    tq = q_ref.shape[0]
    half = tq // 2
    units = []
    for hd in range(n_q):
        if hd in (0, n_q - 1):
            units += [(hd, slice(0, half)), (hd, slice(half, tq))]
        else:
            units.append((hd, slice(0, tq)))
    for hd, rs in units:
        kv = hd // q_per_kv
        qh = q_ref[rs, hd * HEAD_DIM:(hd + 1) * HEAD_DIM]
        kh = k_ref[:, kv * HEAD_DIM:(kv + 1) * HEAD_DIM]
        vh = v_ref[:, kv * HEAD_DIM:(kv + 1) * HEAD_DIM]
        s = lax.dot_general(qh, kh, (((1,), (1,)), ((), ())), preferred_element_type=F32)
        m = jnp.max(s, axis=-1, keepdims=True)
        p = jnp.exp2(s - m)
        l = jnp.sum(p, axis=-1, keepdims=True)
        o = jnp.dot(p.astype(BF16), vh, preferred_element_type=F32)
        acc_ref[rs, hd * HEAD_DIM:(hd + 1) * HEAD_DIM] = o * (1.0 / l)
    o_ref[...] = _rms(acc_ref[...], gain_ref[...]).astype(BF16)


def _attention(q, k, v, gain, *, batch, seq_len, tq, n_q, n_kv):
    T, dq = q.shape
    dkv = k.shape[1]
    nq_tiles = seq_len // tq
    return pl.pallas_call(
        functools.partial(_attn_kernel, n_q=n_q, n_kv=n_kv),
        out_shape=jax.ShapeDtypeStruct((T, dq), BF16),
        grid=(batch, nq_tiles),
        in_specs=[
            pl.BlockSpec((tq, dq), lambda b, i: (b * nq_tiles + i, 0)),
            pl.BlockSpec((seq_len, dkv), lambda b, i: (b, 0)),
            pl.BlockSpec((seq_len, dkv), lambda b, i: (b, 0)),
            pl.BlockSpec((1, dq), lambda b, i: (0, 0)),
        ],
        out_specs=pl.BlockSpec((tq, dq), lambda b, i: (b * nq_tiles + i, 0)),
        scratch_shapes=[pltpu.VMEM((tq, dq), F32)],
        compiler_params=_cparams("parallel", "arbitrary"),
        name="attention",
    )(q, k, v, gain)


def _split3(a):
    hi = a.astype(BF16)
    r1 = a - hi.astype(F32)
    mid = r1.astype(BF16)
    lo = (r1 - mid.astype(F32)).astype(BF16)
    return hi, mid, lo


def _softplus(x):
    return jnp.maximum(x, 0.0) + jnp.log1p(jnp.exp(-jnp.abs(x)))


def _ssd_kernel(x_ref, b_ref, c_ref, z_ref, dt_ref, dtb_ref, alog_ref, dskip_ref, gain_ref,
                o_ref,
                bmt_s, y_s, xd_s, ecs_s, dec_s, st_s, *, n_chunks, heads):
    width = heads * SSD_HEAD_DIM
    pairs = width // LANES
    dec_rows = 2 * SUBLANES

    row = lax.broadcasted_iota(jnp.int32, (CHUNK, CHUNK), 0)
    col = lax.broadcasted_iota(jnp.int32, (CHUNK, CHUNK), 1)
    keep = (col <= row, col >= row)
    r2 = lax.broadcasted_iota(jnp.int32, (2 * CHUNK, CHUNK), 0)
    c2 = lax.broadcasted_iota(jnp.int32, (2 * CHUNK, CHUNK), 1)
    one_if = lambda cond: jnp.where(cond, 1.0, 0.0)
    tri2 = jnp.where(r2 < CHUNK, one_if(c2 <= r2), one_if(c2 >= r2 - CHUNK)).astype(BF16)
    r2t = lax.broadcasted_iota(jnp.int32, (CHUNK, 2 * CHUNK), 0)
    c2t = lax.broadcasted_iota(jnp.int32, (CHUNK, 2 * CHUNK), 1)
    tri2t = jnp.where(c2t < CHUNK, one_if(r2t <= c2t), one_if(r2t >= c2t - CHUNK)).astype(BF16)
    sel_r = lax.broadcasted_iota(jnp.int32, (LANES, 2 * width), 0)
    sel_c = lax.broadcasted_iota(jnp.int32, (LANES, 2 * width), 1)
    head_lane = (sel_c // width) * heads + (sel_c % width) // SSD_HEAD_DIM
    sel2 = jnp.where(sel_r < 4 * heads, one_if(sel_r % (2 * heads) == head_lane), 0.0).astype(BF16)
    lane_c = lax.broadcasted_iota(jnp.int32, (CHUNK, LANES), 1)
    lane_x = lane_c
    neg_a2 = -jnp.exp(alog_ref[...]) * LOG2E

    def expand(src):
        lane = lax.broadcasted_iota(jnp.int32, src.shape, 1)
        hi = src.astype(BF16).astype(F32)
        packed = jnp.where(lane < 2 * heads, hi, pltpu.roll(src - hi, 2 * heads, 1))
        return jnp.dot(packed.astype(BF16), sel2, preferred_element_type=F32)

    def stage_a(c, carry):
        r0 = pl.multiple_of(c * CHUNK, CHUNK)
        xs = x_ref[pl.ds(r0, CHUNK), :]
        cm = c_ref[pl.ds(r0, CHUNK), :]
        bm_t = b_ref[pl.ds(r0, CHUNK), :].astype(F32).T.astype(BF16)
        bmt_s[c] = bm_t

        dt = _softplus(dt_ref[pl.ds(r0, CHUNK), :] + dtb_ref[...])
        a2 = dt * neg_a2
        cs3 = jnp.dot(tri2, jnp.concatenate(_split3(a2), axis=1), preferred_element_type=F32)
        cs2 = cs3[:, :LANES] + cs3[:, LANES:2 * LANES] + cs3[:, 2 * LANES:]
        cst3 = jnp.dot(jnp.concatenate(_split3(a2.T), axis=0), tri2t, preferred_element_type=F32)
        cst2 = cst3[:CHUNK] + cst3[CHUNK:2 * CHUNK] + cst3[2 * CHUNK:]
        dt_t = dt.T

        is_fwd = lane_c < heads
        cs = jnp.where(is_fwd, cs2[:CHUNK], cs2[CHUNK:])
        total = jnp.where(is_fwd[:1], cs2[CHUNK - 1:CHUNK], cs2[CHUNK:CHUNK + 1])
        w = dt * jnp.exp2(total - cs)
        dec = jnp.broadcast_to(jnp.exp2(total), (dec_rows, LANES))
        ex = expand(jnp.concatenate([w, jnp.exp2(cs), dec], axis=0))
        for d in range(2):
            sl = slice(d * width, (d + 1) * width)
            xd_s[d, pl.ds(r0, CHUNK), :] = (xs * ex[:CHUNK, sl]).astype(BF16)
            ecs_s[d, pl.ds(r0, CHUNK), :] = ex[CHUNK:2 * CHUNK, sl]
            dec_s[d, pl.ds(pl.multiple_of(c * SUBLANES, SUBLANES), SUBLANES), :] = (
                ex[2 * CHUNK:2 * CHUNK + SUBLANES, sl])

        x_bf = xs.astype(BF16)
        cb = jnp.dot(cm, bm_t, preferred_element_type=F32)
        pieces = []
        for p in range(pairs):
            xp = x_bf[:, p * LANES:(p + 1) * LANES]
            zero = jnp.zeros_like(xp)
            x_bd = jnp.concatenate([jnp.where(lane_x < SSD_HEAD_DIM, xp, zero),
                                    jnp.where(lane_x >= SSD_HEAD_DIM, xp, zero)], axis=0)
            ms = []
            for d in range(2):
                for hh in (2 * p, 2 * p + 1):
                    j = d * heads + hh
                    diff = (cs2[d * CHUNK:(d + 1) * CHUNK, j:j + 1]
                            - cst2[j:j + 1, d * CHUNK:(d + 1) * CHUNK])
                    lm = jnp.exp2(jnp.where(keep[d], diff, -jnp.inf))
                    ms.append((cb * lm * dt_t[j:j + 1, :]).astype(BF16))
            pieces.append(jnp.dot(jnp.concatenate(ms, axis=1),
                                  jnp.concatenate([x_bd, x_bd], axis=0),
                                  preferred_element_type=F32))
        y_s[pl.ds(r0, CHUNK), :] = jnp.concatenate(pieces, axis=1) + dskip_ref[...] * xs
        return carry

    lax.fori_loop(0, n_chunks, stage_a, 0, unroll=8)

    st_s[...] = jnp.zeros_like(st_s)

    def stage_b(i, carry):
        for d, c in ((0, i), (1, n_chunks - 1 - i)):
            r0 = pl.multiple_of(c * CHUNK, CHUNK)
            state = st_s[d]
            y_off = jnp.dot(c_ref[pl.ds(r0, CHUNK), :], state.astype(BF16),
                            preferred_element_type=F32) * ecs_s[d, pl.ds(r0, CHUNK), :]
            y_s[pl.ds(r0, CHUNK), :] += y_off
            new = jnp.dot(bmt_s[c], xd_s[d, pl.ds(r0, CHUNK), :], preferred_element_type=F32)
            decay = dec_s[d, pl.ds(pl.multiple_of(c * SUBLANES, SUBLANES), SUBLANES), :][:1]
            st_s[d] = state * decay + new
        return carry

    lax.fori_loop(0, n_chunks, stage_b, 0, unroll=8)

    def stage_c(c, carry):
        r0 = pl.multiple_of(c * CHUNK, CHUNK)
        y = y_s[pl.ds(r0, CHUNK), :] * _silu(z_ref[pl.ds(r0, CHUNK), :])
        o_ref[pl.ds(r0, CHUNK), :] = _rms(y, gain_ref[...]).astype(BF16)
        return carry

    lax.fori_loop(0, n_chunks, stage_c, 0, unroll=4)


def _ssd(xs, bm, cm, z, dt, dt_bias, a_log, d_skip, gain, *, batch, seq_len):
    T, d_ssd = z.shape
    width = d_ssd // N_GROUPS
    heads = width // SSD_HEAD_DIM
    n_chunks = seq_len // CHUNK

    def seq(cols):
        return pl.BlockSpec((seq_len, cols), lambda b, g: (b, g))

    def par(cols):
        return pl.BlockSpec((1, cols), lambda b, g: (0, g))

    return pl.pallas_call(
        functools.partial(_ssd_kernel, n_chunks=n_chunks, heads=heads),
        out_shape=jax.ShapeDtypeStruct((T, d_ssd), BF16),
        grid=(batch, N_GROUPS),
        in_specs=[
            seq(width), seq(D_STATE), seq(D_STATE), seq(width), seq(LANES),
            par(LANES), par(LANES), par(width), par(width),
        ],
        out_specs=seq(width),
        scratch_shapes=[
            pltpu.VMEM((n_chunks, D_STATE, CHUNK), BF16),
            pltpu.VMEM((seq_len, width), F32),
            pltpu.VMEM((2, seq_len, width), BF16),
            pltpu.VMEM((2, seq_len, width), F32),
            pltpu.VMEM((2, n_chunks * SUBLANES, width), F32),
            pltpu.VMEM((2, D_STATE, width), F32),
        ],
        compiler_params=_cparams("parallel", "arbitrary"),
        name="ssd",
    )(xs, bm, cm, z, dt, dt_bias, a_log, d_skip, gain)


def _out_kernel(a_ref, s_ref, h_ref, wa_ref, ws_ref, o_ref):
    o_ref[...] = (h_ref[...]
                  + jnp.dot(a_ref[...], wa_ref[...], preferred_element_type=F32)
                  + jnp.dot(s_ref[...], ws_ref[...], preferred_element_type=F32))


def _out_proj(a, s, h, w_out, *, layer, tm):
    T, D = h.shape
    da, ds = a.shape[1], s.shape[1]
    row = lambda i: (i, 0)
    return pl.pallas_call(
        _out_kernel,
        out_shape=jax.ShapeDtypeStruct((T, D), F32),
        grid=(T // tm,),
        in_specs=[
            pl.BlockSpec((tm, da), row),
            pl.BlockSpec((tm, ds), row),
            pl.BlockSpec((tm, D), row),
            pl.BlockSpec((None, da, D), lambda i: (layer, 0, 0)),
            pl.BlockSpec((None, ds, D), lambda i: (layer, da // ds, 0)),
        ],
        out_specs=pl.BlockSpec((tm, D), row),
        compiler_params=_cparams("parallel"),
        name="out_proj",
    )(a, s, h, w_out, w_out)


def _rope_tables(seq_len):
    axis_dim = HEAD_DIM // 2
    t = jnp.arange(seq_len)
    row_pos = (t // GRID_W).astype(F32)
    col_pos = (t % GRID_W).astype(F32)
    inv_freq = ROPE_THETA ** (-jnp.arange(0, axis_dim, 2, dtype=F32) / axis_dim)
    ang = jnp.concatenate([row_pos[:, None] * inv_freq, col_pos[:, None] * inv_freq], axis=-1)
    cos = jnp.concatenate([jnp.cos(ang), jnp.cos(ang)], axis=-1)
    sin = jnp.concatenate([-jnp.sin(ang), jnp.sin(ang)], axis=-1)
    return cos, sin


def _rope_layout(t):
    quarter = HEAD_DIM // 4
    lead = t.shape[:-1]
    t = t.reshape(lead + (-1, 2, 2, quarter))
    return jnp.swapaxes(t, -3, -2).reshape(lead + (-1,))


def _group_dt_columns(t, n_heads):
    hpg = n_heads // N_GROUPS
    lead = t.shape[:-1]
    t = t.reshape(lead + (2, N_GROUPS, hpg))
    t = jnp.moveaxis(t, -3, -2).reshape(lead + (N_GROUPS, 2 * hpg))
    t = jnp.pad(t, [(0, 0)] * (len(lead) + 1) + [(0, LANES - 2 * hpg)])
    return t.reshape(lead + (N_GROUPS * LANES,))


def kernel(x, ffn1_norm, ffn1_w_gu, ffn1_w_down, mix_norm, w_in, conv_w, conv_b, dt_bias, a_log, d_skip, q_norm, k_norm, attn_out_norm, ssd_out_norm, w_out, ffn2_norm, ffn2_w_gu, ffn2_w_down, final_norm):
    batch, seq_len, d_model = x.shape
    depth = ffn1_norm.shape[0]
    d_attn = attn_out_norm.shape[1]
    d_ssd = ssd_out_norm.shape[1]
    n_ssd_heads = d_skip.shape[1]
    n_q = d_attn // HEAD_DIM
    d_kv = N_KV_HEADS * HEAD_DIM
    d_xbc = conv_w.shape[2]
    d_qkv = d_attn + 2 * d_kv
    d_dt = N_GROUPS * LANES

    T = batch * seq_len
    tm = _token_tile(seq_len)
    tm_ffn, tf = _ffn_tiles(T, ffn1_w_down.shape[1])
    h = x.reshape(T, d_model)
    cos, sin = _rope_tables(seq_len)
    row = lambda v: v.reshape(1, -1)
    final_gain = row(final_norm)

    ffn_stacks = ((ffn1_w_gu, ffn1_w_down), (ffn2_w_gu, ffn2_w_down))
    ffn_w = [ffn1_w_gu[0].astype(BF16), ffn1_w_down[0].astype(BF16)]

    def next_ffn_weights(call):
        layer, which = divmod(call, 2)
        return [(stack, layer) for stack in ffn_stacks[which]] if layer < depth else []

    d_qk = d_attn + d_kv
    w_qkv = jnp.concatenate([_rope_layout(w_in[:, :, :d_qk]), w_in[:, :, d_qk:d_qkv]], axis=-1).astype(BF16)
    q_gain, k_gain = _rope_layout(q_norm), _rope_layout(k_norm)
    z0 = d_qkv
    xbc0 = z0 + d_ssd
    dt0 = xbc0 + d_xbc
    w_ssd = jnp.concatenate([w_in[:, :, xbc0:dt0], w_in[:, :, z0:xbc0],
                             _group_dt_columns(w_in[:, :, dt0:], n_ssd_heads)], axis=-1).astype(BF16)
    w_o = w_out.astype(BF16)
    dt_bias_g = _group_dt_columns(dt_bias.reshape(depth, -1), n_ssd_heads)
    a_log_g = _group_dt_columns(a_log.reshape(depth, -1), n_ssd_heads)
    d_skip_e = jnp.repeat(d_skip, SSD_HEAD_DIM, axis=-1)

    for i in range(depth):
        last = i == depth - 1
        h, ffn_w = _ffn(h, row(ffn1_norm[i]), ffn_w[0], ffn_w[1], final_gain, next_ffn_weights(2 * i + 1),
                        tm=tm_ffn, tf=tf, final_norm=False)

        gain = row(mix_norm[i])
        q, k, v = _qkv_proj(h, gain, w_qkv, cos, sin, row(q_gain[i]), row(k_gain[i]),
                            layer=i, tm=tm, seq_len=seq_len, n_q=n_q, n_kv=N_KV_HEADS)
        xs, bm, cm, z, dt = _ssd_proj(h, gain, w_ssd, conv_w[i], row(conv_b[i]), layer=i, tm=tm,
                                      seq_len=seq_len, d_x=d_ssd, d_bc=(d_xbc - d_ssd) // 2,
                                      d_z=d_ssd, d_dt=d_dt)
        a_out = _attention(q, k, v, row(attn_out_norm[i]), batch=batch, seq_len=seq_len,
                           tq=tm, n_q=n_q, n_kv=N_KV_HEADS)
        s_out = _ssd(xs, bm, cm, z, dt, row(dt_bias_g[i]), row(a_log_g[i]),
                     row(d_skip_e[i]), row(ssd_out_norm[i]), batch=batch, seq_len=seq_len)
        h = _out_proj(a_out, s_out, h, w_o, layer=i, tm=tm)

        h, ffn_w = _ffn(h, row(ffn2_norm[i]), ffn_w[0], ffn_w[1], final_gain, next_ffn_weights(2 * i + 2),
                        tm=tm_ffn, tf=tf, final_norm=last)
    return h.reshape(batch, seq_len, d_model)
```
